```python
import jax, jax.numpy as jnp
from jax import lax
import numpy as np

D_MODEL = 1024
BATCH = 16
SEQ = 2048
DEPTH = 2
DEC_BATCH = 4
DEC_SEQ = 8192
PAST_LEN = 128

HEAD_DIM = 64
A_HEADS = 8
A_KV_HEADS = 2
A_GROUP = A_HEADS // A_KV_HEADS
A_RADIUS = 128
B_HEADS = 8
B_CONFIGS = ((128, 1), (512, 4), (2048, 16))
M_HEADS = 4
M_HEAD_DIM = 128
N_MEM = 256
N_BRANCH = 3
BRANCH_WIDTH = 512
D_FF = 2816
LN_EPS = 1e-5
ALPHA = (2 * DEPTH) ** 0.25
BETA = (8 * DEPTH) ** -0.25
NEG_INF = -1e30

QA_COLS = A_HEADS * HEAD_DIM
KVA_COLS = 2 * A_KV_HEADS * HEAD_DIM
QKVB_COLS = 3 * B_HEADS * HEAD_DIM
QM_COLS = M_HEADS * M_HEAD_DIM
GATE_COLS = N_BRANCH * D_MODEL
IN_COLS = QA_COLS + KVA_COLS + QKVB_COLS + QM_COLS + GATE_COLS

kernel_name = "hybrid_bidir_encoder"


def _alibi_slopes(n):
    return jnp.asarray(2.0 ** (-8.0 * np.arange(1, n + 1) / n), dtype=jnp.float32)


def _layer_norm(x, g, b):
    xf = x.astype(jnp.float32)
    mu = xf.mean(-1, keepdims=True)
    var = jnp.square(xf - mu).mean(-1, keepdims=True)
    return ((xf - mu) * lax.rsqrt(var + LN_EPS) * g.astype(jnp.float32) + b.astype(jnp.float32)).astype(x.dtype)


def _swiglu(x, w_in, w_out):
    gate, up = jnp.split(x @ w_in, 2, axis=-1)
    return (jax.nn.silu(gate) * up) @ w_out


def _banded_attention(q, k, v, radius, slopes, dist_scale, sink=None):
    bsz, length, n_kv, grp, dh = q.shape
    blk = radius
    nb = -(-length // blk)
    pad = nb * blk - length
    qb = jnp.pad(q, ((0, 0), (0, pad), (0, 0), (0, 0), (0, 0))).reshape(bsz, nb, blk, n_kv, grp, dh)

    def _blocks(t):
        tp = jnp.pad(t, ((0, 0), (blk, pad + blk), (0, 0), (0, 0))).reshape(bsz, nb + 2, blk, n_kv, dh)
        return jnp.concatenate([tp[:, :-2], tp[:, 1:-1], tp[:, 2:]], axis=2)

    kb, vb = _blocks(k), _blocks(v)
    s = jnp.einsum('bnqhgd,bnkhd->bnhgqk', qb, kb, preferred_element_type=jnp.float32) * (dh ** -0.5)
    qi = jnp.arange(blk)[:, None]
    ki = jnp.arange(3 * blk)[None, :]
    rel = ki - blk - qi
    dist = jnp.abs(rel).astype(jnp.float32) * dist_scale
    key_pos = jnp.arange(nb)[:, None, None] * blk - blk + ki[None]
    valid = (jnp.abs(rel)[None] <= radius) & (key_pos >= 0) & (key_pos < length)
    s = s - slopes[:, :, None, None] * dist
    s = jnp.where(valid[None, :, None, None], s, NEG_INF)
    m = s.max(-1)
    if sink is not None:
        m = jnp.maximum(m, sink[:, :, None])
    p = jnp.exp(s - m[..., None])
    denom = p.sum(-1)
    total = denom + jnp.exp(sink[:, :, None] - m) if sink is not None else denom
    o = jnp.einsum('bnhgqk,bnkhd->bnqhgd', p, vb.astype(jnp.float32))
    o = o / total.transpose(0, 1, 4, 2, 3)[..., None]
    o = o.reshape(bsz, nb * blk, n_kv, grp, dh)[:, :length]
    lse = (m + jnp.log(denom)).transpose(0, 1, 4, 2, 3).reshape(bsz, nb * blk, n_kv, grp)[:, :length]
    return o, lse


def _dilated_attention(q, k, v, slopes):
    bsz, seq, nh, dh = q.shape
    outs, lses = [], []
    for window, dil in B_CONFIGS:
        radius = window // (2 * dil)
        sub_len = seq // dil

        def _sub(t):
            return t.reshape(bsz, sub_len, dil, nh, dh).transpose(0, 2, 1, 3, 4).reshape(bsz * dil, sub_len, nh, dh)

        o, lse = _banded_attention(_sub(q)[:, :, :, None], _sub(k), _sub(v), radius, slopes[:, None], dil)
        outs.append(o.reshape(bsz, dil, sub_len, nh, dh).transpose(0, 2, 1, 3, 4).reshape(bsz, seq, nh, dh))
        lses.append(lse.reshape(bsz, dil, sub_len, nh).transpose(0, 2, 1, 3).reshape(bsz, seq, nh))
    w = jax.nn.softmax(jnp.stack(lses), axis=0)
    return jnp.einsum('cbsh,cbshd->bshd', w, jnp.stack(outs))


def _memory_attention(q, mem, w_mem_kv):
    bsz, n_mem, _ = mem.shape
    km, vm = jnp.split(mem @ w_mem_kv, 2, axis=-1)
    km = km.reshape(bsz, n_mem, M_HEADS, M_HEAD_DIM)
    vm = vm.reshape(bsz, n_mem, M_HEADS, M_HEAD_DIM)
    s = jnp.einsum('bshd,bmhd->bhsm', q, km, preferred_element_type=jnp.float32) * (M_HEAD_DIM ** -0.5)
    p = jax.nn.softmax(s, axis=-1)
    return jnp.einsum('bhsm,bmhd->bshd', p, vm.astype(jnp.float32))


def _token_mix(x, mem, w_in, w_mem_kv, sink_a, w_branch, w_out):
    bsz, seq, _ = x.shape
    h = x @ w_in
    q_a, kv_a, qkv_b, q_m, gates = jnp.split(
        h, [QA_COLS, QA_COLS + KVA_COLS, QA_COLS + KVA_COLS + QKVB_COLS, QA_COLS + KVA_COLS + QKVB_COLS + QM_COLS], axis=-1)
    k_a, v_a = jnp.split(kv_a.reshape(bsz, seq, 2, A_KV_HEADS, HEAD_DIM), 2, axis=2)
    o_a, _ = _banded_attention(q_a.reshape(bsz, seq, A_KV_HEADS, A_GROUP, HEAD_DIM), k_a[:, :, 0], v_a[:, :, 0], A_RADIUS,
                               _alibi_slopes(A_HEADS).reshape(A_KV_HEADS, A_GROUP), 1,
                               sink=sink_a.astype(jnp.float32).reshape(A_KV_HEADS, A_GROUP))
    qkv_b = qkv_b.reshape(bsz, seq, 3, B_HEADS, HEAD_DIM)
    o_b = _dilated_attention(qkv_b[:, :, 0], qkv_b[:, :, 1], qkv_b[:, :, 2], _alibi_slopes(B_HEADS))
    o_m = _memory_attention(q_m.reshape(bsz, seq, M_HEADS, M_HEAD_DIM), mem, w_mem_kv)
    branches = jnp.stack([o_a.reshape(bsz, seq, BRANCH_WIDTH), o_b.reshape(bsz, seq, BRANCH_WIDTH),
                          o_m.reshape(bsz, seq, BRANCH_WIDTH)], axis=2).astype(x.dtype)
    proj = jnp.einsum('bsie,ied->bsid', branches, w_branch)
    g = jax.nn.sigmoid(gates).reshape(bsz, seq, N_BRANCH, D_MODEL)
    return (g * proj).sum(axis=2) @ w_out


def setup_inputs(seed: int = 0) -> dict:
    key = jax.random.key(seed)
    ks = jax.random.split(key, 20)
    f32 = jnp.float32

    def nrm(k, shape, scale):
        return jax.random.normal(k, shape, f32) * scale

    return {
        "x_prompt": nrm(ks[0], (BATCH, SEQ, D_MODEL), 1.0),
        "x_sample": nrm(ks[1], (DEC_BATCH, DEC_SEQ, D_MODEL), 1.0),
        "mem_prompt": nrm(ks[2], (BATCH, N_MEM, D_MODEL), 1.0),
        "mem_sample": nrm(ks[3], (DEC_BATCH, N_MEM, D_MODEL), 1.0),
        "ffn1_w_in": nrm(ks[4], (DEPTH, D_MODEL, 2 * D_FF), D_MODEL ** -0.5),
        "ffn1_w_out": nrm(ks[5], (DEPTH, D_FF, D_MODEL), BETA * D_FF ** -0.5),
        "ln1_g": 1.0 + nrm(ks[6], (DEPTH, D_MODEL), 0.02),
        "ln1_b": nrm(ks[7], (DEPTH, D_MODEL), 0.02),
        "w_in": nrm(ks[8], (DEPTH, D_MODEL, IN_COLS), D_MODEL ** -0.5),
        "w_mem_kv": nrm(ks[9], (DEPTH, D_MODEL, 2 * QM_COLS), D_MODEL ** -0.5),
        "sink_a": nrm(ks[10], (DEPTH, A_HEADS), 1.0),
        "w_branch": nrm(ks[11], (DEPTH, N_BRANCH, BRANCH_WIDTH, D_MODEL), BRANCH_WIDTH ** -0.5),
        "w_out": nrm(ks[12], (DEPTH, D_MODEL, D_MODEL), BETA * D_MODEL ** -0.5),
        "ln2_g": 1.0 + nrm(ks[13], (DEPTH, D_MODEL), 0.02),
        "ln2_b": nrm(ks[14], (DEPTH, D_MODEL), 0.02),
        "ffn2_w_in": nrm(ks[15], (DEPTH, D_MODEL, 2 * D_FF), D_MODEL ** -0.5),
        "ffn2_w_out": nrm(ks[16], (DEPTH, D_FF, D_MODEL), BETA * D_FF ** -0.5),
        "ln3_g": 1.0 + nrm(ks[17], (DEPTH, D_MODEL), 0.02),
        "ln3_b": nrm(ks[18], (DEPTH, D_MODEL), 0.02),
    }


def reference(x_prompt, x_sample, mem_prompt, mem_sample, ffn1_w_in, ffn1_w_out, ln1_g, ln1_b, w_in, w_mem_kv,
              sink_a, w_branch, w_out, ln2_g, ln2_b, ffn2_w_in, ffn2_w_out, ln3_g, ln3_b):
    def trunk(x, mem):
        for l in range(DEPTH):
            x = _layer_norm(ALPHA * x + 0.5 * _swiglu(x, ffn1_w_in[l], ffn1_w_out[l]), ln1_g[l], ln1_b[l])
            x = _layer_norm(ALPHA * x + _token_mix(x, mem, w_in[l], w_mem_kv[l], sink_a[l], w_branch[l], w_out[l]),
                            ln2_g[l], ln2_b[l])
            x = _layer_norm(ALPHA * x + 0.5 * _swiglu(x, ffn2_w_in[l], ffn2_w_out[l]), ln3_g[l], ln3_b[l])
        return x

    y_prompt = trunk(x_prompt, mem_prompt)
    y_sample = trunk(x_sample, mem_sample)
    return (y_prompt, y_sample)
```

```python
import functools

import jax
import jax.numpy as jnp
from jax import lax
from jax.experimental import pallas as pl
from jax.experimental.pallas import tpu as pltpu

F32 = jnp.float32
BF16 = jnp.bfloat16

D_MODEL = 1024
DEPTH = 2
HEAD_DIM = 64
A_HEADS = 8
A_KV_HEADS = 2
A_GROUP = A_HEADS // A_KV_HEADS
A_RADIUS = 128
B_HEADS = 8
B_DILATIONS = (1, 4, 16)
B_RADIUS = 64
M_HEADS = 4
M_HEAD_DIM = 128
N_MEM = 256
N_BRANCH = 3
BRANCH_WIDTH = 512
D_FF = 2816
LN_EPS = 1e-5
ALPHA = (2 * DEPTH) ** 0.25
NEG_INF = -1e30

QA_COLS = A_HEADS * HEAD_DIM
KVA_COLS = 2 * A_KV_HEADS * HEAD_DIM
QKVB_COLS = 3 * B_HEADS * HEAD_DIM
QM_COLS = M_HEADS * M_HEAD_DIM
GATE_COLS = N_BRANCH * D_MODEL
QM_START = QA_COLS + KVA_COLS + QKVB_COLS
GATE_START = QM_START + QM_COLS

LANES = 128
MAX_DIL = 16
B_PAIRS = B_HEADS * HEAD_DIM // LANES
A_SLABS = QA_COLS // LANES
PROJ_COLS = QA_COLS + KVA_COLS + QKVB_COLS

FFN_ROWS = 512
FFN_CHUNK = 256
PROJ_ROWS = 512
MERGE_ROWS = 512
A_TILE = 1024
A_BLK = A_RADIUS
SUPER = 128

VMEM_LIMIT = 56 * 1024 * 1024


def _resident(shape):
    zeros = (0,) * len(shape)
    return pl.BlockSpec(shape, lambda *_: zeros, pipeline_mode=pl.Buffered(1))


def _params(n_axes):
    return pltpu.CompilerParams(dimension_semantics=("parallel",) * n_axes,
                                vmem_limit_bytes=VMEM_LIMIT)


def _layer_norm(y, g, b):
    mu = jnp.mean(y, axis=-1, keepdims=True)
    d = y - mu
    var = jnp.mean(d * d, axis=-1, keepdims=True)
    return d * lax.rsqrt(var + LN_EPS) * g + b


def _dot(a, b):
    return jnp.dot(a, b, preferred_element_type=F32)


def _dot_nt(a, b):
    return lax.dot_general(a, b, (((1,), (1,)), ((), ())), preferred_element_type=F32)


def _ffn_ln_kernel(x_ref, win_ref, wout_ref, g_ref, b_ref, o_ref):
    x = x_ref[...]
    xb = x.astype(BF16)
    acc = None
    for c in range(D_FF // FFN_CHUNK):
        lo = c * FFN_CHUNK
        gate = _dot(xb, win_ref[:, lo:lo + FFN_CHUNK])
        up = _dot(xb, win_ref[:, D_FF + lo:D_FF + lo + FFN_CHUNK])
        act = (gate * jax.nn.sigmoid(gate) * up).astype(BF16)
        part = _dot(act, wout_ref[lo:lo + FFN_CHUNK, :])
        acc = part if acc is None else acc + part
    o_ref[...] = _layer_norm(ALPHA * x + 0.5 * acc, g_ref[...], b_ref[...])


def _ffn_ln(x2d, w_in, w_out, g, b):
    rows = x2d.shape[0]
    return pl.pallas_call(
        _ffn_ln_kernel,
        grid=(rows // FFN_ROWS,),
        in_specs=[
            pl.BlockSpec((FFN_ROWS, D_MODEL), lambda i: (i, 0)),
            _resident((D_MODEL, 2 * D_FF)),
            _resident((D_FF, D_MODEL)),
            _resident((1, D_MODEL)),
            _resident((1, D_MODEL)),
        ],
        out_specs=pl.BlockSpec((FFN_ROWS, D_MODEL), lambda i: (i, 0)),
        out_shape=jax.ShapeDtypeStruct((rows, D_MODEL), F32),
        compiler_params=_params(1),
        name="ffn_ln",
    )(x2d, w_in, w_out, g, b)


def _in_proj_kernel(x_ref, w_ref, qa_ref, ka_ref, va_ref, qb_ref, kb_ref, vb_ref):
    xb = x_ref[0].astype(BF16)
    qa_ref[0] = _dot(xb, w_ref[:, 0:QA_COLS]).astype(BF16)
    kva = _dot(xb, w_ref[:, QA_COLS:QA_COLS + KVA_COLS]).astype(BF16)
    ka_ref[0] = kva[:, :LANES]
    va_ref[0] = kva[:, LANES:]
    base = QA_COLS + KVA_COLS
    for n, ref in enumerate((qb_ref, kb_ref, vb_ref)):
        lo = base + n * BRANCH_WIDTH
        res = _dot(xb, w_ref[:, lo:lo + BRANCH_WIDTH]).astype(BF16)
        for hp in range(B_PAIRS):
            ref[0, hp] = res[:, hp * LANES:(hp + 1) * LANES]


def _in_proj(x, w):
    bsz, seq, _ = x.shape
    rows = PROJ_ROWS
    tok = lambda cols: pl.BlockSpec((1, rows, cols), lambda b, i: (b, i, 0))
    pair = pl.BlockSpec((1, B_PAIRS, rows, LANES), lambda b, i: (b, 0, i, 0))
    pair_shape = jax.ShapeDtypeStruct((bsz, B_PAIRS, seq, LANES), BF16)
    return pl.pallas_call(
        _in_proj_kernel,
        grid=(bsz, seq // rows),
        in_specs=[tok(D_MODEL), _resident((D_MODEL, PROJ_COLS))],
        out_specs=[tok(QA_COLS), tok(LANES), tok(LANES), pair, pair, pair],
        out_shape=[
            jax.ShapeDtypeStruct((bsz, seq, QA_COLS), BF16),
            jax.ShapeDtypeStruct((bsz, seq, LANES), BF16),
            jax.ShapeDtypeStruct((bsz, seq, LANES), BF16),
            pair_shape, pair_shape, pair_shape,
        ],
        compiler_params=_params(2),
        name="in_proj",
    )(x, w)


def _mem_kv_kernel(mem_ref, w_ref, km_ref, vm_ref):
    res = _dot(mem_ref[0].astype(BF16), w_ref[...]).astype(BF16)
    km_ref[0] = res[:, :QM_COLS]
    vm_ref[0] = res[:, QM_COLS:]


def _mem_kv(mem, w):
    bsz = mem.shape[0]
    out = pl.BlockSpec((1, N_MEM, QM_COLS), lambda b: (b, 0, 0))
    shape = jax.ShapeDtypeStruct((bsz, N_MEM, QM_COLS), BF16)
    return pl.pallas_call(
        _mem_kv_kernel,
        grid=(bsz,),
        in_specs=[pl.BlockSpec((1, N_MEM, D_MODEL), lambda b: (b, 0, 0)),
                  _resident((D_MODEL, 2 * QM_COLS))],
        out_specs=[out, out],
        out_shape=[shape, shape],
        compiler_params=_params(1),
        name="mem_kv",
    )(mem, w)


def _attn_a_kernel(q_ref, kp_ref, kc_ref, kn_ref, vp_ref, vc_ref, vn_ref, sink_ref, bias_ref,
                   o_ref, kbuf, vbuf, *, n_blocks):
    tile = pl.program_id(1)
    blocks_per_tile = A_TILE // A_BLK
    for buf, prev, cur, nxt in ((kbuf, kp_ref, kc_ref, kn_ref), (vbuf, vp_ref, vc_ref, vn_ref)):
        buf[0:A_BLK] = prev[0]
        buf[A_BLK:A_BLK + A_TILE] = cur[0]
        buf[A_BLK + A_TILE:] = nxt[0]
    low_half = lax.broadcasted_iota(jnp.int32, (1, LANES), 1) < HEAD_DIM

    def block(j, carry):
        row = pl.multiple_of(j * A_BLK, A_BLK)
        gblk = tile * blocks_per_tile + j
        variant = jnp.where(gblk == 0, 1, jnp.where(gblk == n_blocks - 1, 2, 0))
        q4 = q_ref[0, pl.ds(row, A_BLK), :]
        k3 = kbuf[pl.ds(row, 3 * A_BLK), :]
        v3 = vbuf[pl.ds(row, 3 * A_BLK), :]
        outs = []
        for g in range(A_KV_HEADS):
            keep = low_half if g == 0 else jnp.logical_not(low_half)
            lhs = jnp.concatenate(
                [jnp.where(keep, q4[:, i * LANES:(i + 1) * LANES], jnp.zeros((), BF16))
                 for i in range(A_SLABS)], axis=0)
            s = _dot_nt(lhs, k3) + bias_ref[variant, g]
            sink = sink_ref[g]
            m = jnp.maximum(jnp.max(s, axis=-1, keepdims=True), sink)
            p = jnp.exp(s - m)
            total = jnp.sum(p, axis=-1, keepdims=True) + jnp.exp(sink - m)
            outs.append(_dot(p.astype(BF16), v3) / total)
        for i in range(A_SLABS):
            rows = slice(i * A_BLK, (i + 1) * A_BLK)
            slab = jnp.where(low_half, outs[0][rows], outs[1][rows])
            o_ref[0, pl.ds(row, A_BLK), i * LANES:(i + 1) * LANES] = slab.astype(BF16)
        return carry

    lax.fori_loop(0, blocks_per_tile, block, 0)


def _attn_a(qa, ka, va, sink_col, bias):
    bsz, seq, _ = qa.shape
    n_blocks = seq // A_BLK
    per_tile = A_TILE // A_BLK
    cur = pl.BlockSpec((1, A_TILE, LANES), lambda b, i: (b, i, 0))
    prev = pl.BlockSpec((1, A_BLK, LANES), lambda b, i: (b, jnp.maximum(i * per_tile - 1, 0), 0))
    nxt = pl.BlockSpec((1, A_BLK, LANES),
                       lambda b, i: (b, jnp.minimum((i + 1) * per_tile, n_blocks - 1), 0))
    qspec = pl.BlockSpec((1, A_TILE, QA_COLS), lambda b, i: (b, i, 0))
    return pl.pallas_call(
        functools.partial(_attn_a_kernel, n_blocks=n_blocks),
        grid=(bsz, seq // A_TILE),
        in_specs=[qspec, prev, cur, nxt, prev, cur, nxt,
                  _resident(sink_col.shape), _resident(bias.shape)],
        out_specs=qspec,
        out_shape=jax.ShapeDtypeStruct((bsz, seq, QA_COLS), BF16),
        scratch_shapes=[pltpu.VMEM((A_TILE + 2 * A_BLK, LANES), BF16),
                        pltpu.VMEM((A_TILE + 2 * A_BLK, LANES), BF16)],
        compiler_params=_params(2),
        name="attn_a",
    )(qa, ka, ka, ka, va, va, va, sink_col, bias)


def _attn_a_bias():
    slopes = 2.0 ** (-8.0 * jnp.arange(1, A_HEADS + 1, dtype=F32) / A_HEADS)
    qrow = jnp.arange(A_BLK)[:, None]
    kcol = jnp.arange(3 * A_BLK)[None, :]
    rel = kcol - A_BLK - qrow
    band = jnp.abs(rel) <= A_RADIUS
    dist = jnp.abs(rel).astype(F32)
    variants = []
    for left, right in ((False, False), (True, False), (False, True)):
        valid = band
        if left:
            valid = valid & (kcol >= A_BLK)
        if right:
            valid = valid & (kcol < 2 * A_BLK)
        per_head = jnp.where(valid[None], -slopes[:, None, None] * dist[None], NEG_INF)
        variants.append(per_head.reshape(A_KV_HEADS, A_GROUP * A_BLK, 3 * A_BLK))
    return jnp.stack(variants)


def _pair_attend(q2, k2, v2, bias_of_head, low_half):
    outs, sums, maxes = [], [], []
    for hh in range(2):
        keep = low_half if hh == 0 else jnp.logical_not(low_half)
        qh = jnp.where(keep, q2, jnp.zeros((), BF16))
        s = _dot_nt(qh, k2) + bias_of_head(hh)
        m = jnp.max(s, axis=-1, keepdims=True)
        p = jnp.exp(s - m)
        sums.append(jnp.sum(p, axis=-1, keepdims=True))
        maxes.append(m)
        outs.append(_dot(p.astype(BF16), v2))
    denom = jnp.where(low_half, sums[0], sums[1])
    out = jnp.where(low_half, outs[0], outs[1]) / denom
    lse = jnp.where(low_half, maxes[0], maxes[1]) + jnp.log(denom)
    return out, lse


def _attn_b_kernel(q_ref, k_ref, v_ref, b1_ref, b4_ref, b16_ref, o_ref, qs, ks, vs, osc, lsc, *, rows):
    pad = B_RADIUS
    zeros = jnp.zeros((pad, LANES), F32)
    for r in range(MAX_DIL):
        lanes = slice(r * LANES, (r + 1) * LANES)
        qs[r] = q_ref[0, 0, :, lanes].astype(F32)
        for src, dst in ((k_ref, ks), (v_ref, vs)):
            dst[r, 0:pad] = zeros
            dst[r, pad:pad + rows] = src[0, 0, :, lanes].astype(F32)
            dst[r, pad + rows:] = zeros
    low_half = lax.broadcasted_iota(jnp.int32, (1, LANES), 1) < HEAD_DIM

    def edge_variant(first, last):
        return first.astype(jnp.int32) + 2 * last.astype(jnp.int32)

    def super_tile(st, carry):
        u0 = pl.multiple_of(st * SUPER, SUPER)

        var16 = edge_variant(u0 == 0, u0 == rows - SUPER)

        def d16(r, c):
            q2 = qs[r, pl.ds(u0, SUPER), :].astype(BF16)
            k2 = ks[r, pl.ds(u0, SUPER + 2 * pad), :].astype(BF16)
            v2 = vs[r, pl.ds(u0, SUPER + 2 * pad), :].astype(BF16)
            out, lse = _pair_attend(q2, k2, v2, lambda hh: b16_ref[0, var16, hh], low_half)
            osc[2, r] = out
            lsc[2, r] = lse
            return c

        lax.fori_loop(0, MAX_DIL, d16, 0)

        def d4(n, c):
            r4 = n // 4
            u1 = pl.multiple_of(u0 + (n % 4) * 32, 32)
            var4 = edge_variant(u1 == 0, u1 == rows - 32)
            q2 = jnp.concatenate([qs[4 * a + r4, pl.ds(u1, 32), :] for a in range(4)], axis=0)
            kstart = pl.multiple_of(u1 + pad - 16, 16)
            k2 = jnp.concatenate([ks[4 * a + r4, pl.ds(kstart, 64), :] for a in range(4)], axis=0)
            v2 = jnp.concatenate([vs[4 * a + r4, pl.ds(kstart, 64), :] for a in range(4)], axis=0)
            out, lse = _pair_attend(q2.astype(BF16), k2.astype(BF16), v2.astype(BF16),
                                    lambda hh: b4_ref[0, var4, hh], low_half)
            dst = pl.multiple_of((n % 4) * 32, 32)
            for a in range(4):
                osc[1, 4 * a + r4, pl.ds(dst, 32), :] = out[a * 32:(a + 1) * 32]
                lsc[1, 4 * a + r4, pl.ds(dst, 32), :] = lse[a * 32:(a + 1) * 32]
            return c

        lax.fori_loop(0, 16, d4, 0)

        def d1(n, c):
            u1 = pl.multiple_of(u0 + n * 16, 16)
            var1 = edge_variant(u1 == 0, u1 == rows - 16)
            q2 = jnp.concatenate([qs[r, pl.ds(u1, 16), :] for r in range(MAX_DIL)], axis=0)
            kstart = pl.multiple_of(u1 + pad - 8, 8)
            k2 = jnp.concatenate([ks[r, pl.ds(kstart, 32), :] for r in range(MAX_DIL)], axis=0)
            v2 = jnp.concatenate([vs[r, pl.ds(kstart, 32), :] for r in range(MAX_DIL)], axis=0)
            out, lse = _pair_attend(q2.astype(BF16), k2.astype(BF16), v2.astype(BF16),
                                    lambda hh: b1_ref[0, var1, hh], low_half)
            dst = pl.multiple_of(n * 16, 16)
            for r in range(MAX_DIL):
                osc[0, r, pl.ds(dst, 16), :] = out[r * 16:(r + 1) * 16]
                lsc[0, r, pl.ds(dst, 16), :] = lse[r * 16:(r + 1) * 16]
            return c

        lax.fori_loop(0, SUPER // 16, d1, 0)

        for r in range(MAX_DIL):
            l0, l1, l2 = lsc[0, r], lsc[1, r], lsc[2, r]
            m = jnp.maximum(jnp.maximum(l0, l1), l2)
            e0, e1, e2 = jnp.exp(l0 - m), jnp.exp(l1 - m), jnp.exp(l2 - m)
            mixed = (e0 * osc[0, r] + e1 * osc[1, r] + e2 * osc[2, r]) / (e0 + e1 + e2)
            o_ref[0, 0, pl.ds(u0, SUPER), r * LANES:(r + 1) * LANES] = mixed.astype(BF16)
        return carry

    lax.fori_loop(0, rows // SUPER, super_tile, 0)


def _attn_b(qb, kb, vb, biases):
    bsz, _, seq, _ = qb.shape
    rows = seq // MAX_DIL
    wide = MAX_DIL * LANES
    view = lambda t: t.reshape(bsz, B_PAIRS, rows, wide)
    spec = pl.BlockSpec((1, 1, rows, wide), lambda hp, b: (b, hp, 0, 0))
    bias_specs = [pl.BlockSpec((1,) + t.shape[1:], lambda hp, b: (hp, 0, 0, 0, 0)) for t in biases]
    out = pl.pallas_call(
        functools.partial(_attn_b_kernel, rows=rows),
        grid=(B_PAIRS, bsz),
        in_specs=[spec, spec, spec] + bias_specs,
        out_specs=spec,
        out_shape=jax.ShapeDtypeStruct((bsz, B_PAIRS, rows, wide), BF16),
        scratch_shapes=[
            pltpu.VMEM((MAX_DIL, rows, LANES), F32),
            pltpu.VMEM((MAX_DIL, rows + 2 * B_RADIUS, LANES), F32),
            pltpu.VMEM((MAX_DIL, rows + 2 * B_RADIUS, LANES), F32),
            pltpu.VMEM((len(B_DILATIONS), MAX_DIL, SUPER, LANES), F32),
            pltpu.VMEM((len(B_DILATIONS), MAX_DIL, SUPER, LANES), F32),
        ],
        compiler_params=_params(2),
        name="attn_b",
    )(view(qb), view(kb), view(vb), *biases)
    return out.reshape(bsz, B_PAIRS, seq, LANES)


def _attn_b_biases():
    slopes = 2.0 ** (-8.0 * jnp.arange(1, B_HEADS + 1, dtype=F32) / B_HEADS)
    slopes = slopes.reshape(B_PAIRS, 1, 2, 1, 1)

    def build(dil, q_pos, k_pos, k_row, tile_rows):
        rel = k_pos[None, :] - q_pos[:, None]
        band = jnp.abs(rel) <= B_RADIUS
        dist = jnp.abs(rel).astype(F32) * dil
        variants = []
        for var in range(4):
            valid = band
            if var & 1:
                valid = valid & (k_row >= 0)[None, :]
            if var & 2:
                valid = valid & (k_row < tile_rows)[None, :]
            variants.append(valid)
        valid = jnp.stack(variants)[None, :, None]
        return jnp.where(valid, -slopes * dist[None, None, None], NEG_INF)

    ar = jnp.arange
    b16 = build(16, ar(SUPER), ar(SUPER + 2 * B_RADIUS) - B_RADIUS,
                ar(SUPER + 2 * B_RADIUS) - B_RADIUS, SUPER)
    a4 = ar(4)[:, None]
    q4 = (4 * ar(32)[None, :] + a4).reshape(-1)
    k4 = (4 * (ar(64)[None, :] - 16) + a4).reshape(-1)
    k4_row = jnp.broadcast_to(ar(64)[None, :] - 16, (4, 64)).reshape(-1)
    b4 = build(4, q4, k4, k4_row, 32)
    r16 = ar(MAX_DIL)[:, None]
    q1 = (16 * ar(16)[None, :] + r16).reshape(-1)
    k1 = (16 * (ar(32)[None, :] - 8) + r16).reshape(-1)
    k1_row = jnp.broadcast_to(ar(32)[None, :] - 8, (MAX_DIL, 32)).reshape(-1)
    b1 = build(1, q1, k1, k1_row, 16)
    return b1, b4, b16


def _merge_ln_kernel(x_ref, oa_ref, ob_ref, km_ref, vm_ref, wqm_ref, wg_ref, wbr_ref, wo_ref,
                     g_ref, b_ref, o_ref):
    x = x_ref[0]
    xb = x.astype(BF16)
    qm = _dot(xb, wqm_ref[...]).astype(BF16)
    heads = []
    for h in range(M_HEADS):
        lanes = slice(h * M_HEAD_DIM, (h + 1) * M_HEAD_DIM)
        s = _dot_nt(qm[:, lanes], km_ref[0, :, lanes]) * (M_HEAD_DIM ** -0.5)
        p = jnp.exp(s - jnp.max(s, axis=-1, keepdims=True))
        denom = jnp.sum(p, axis=-1, keepdims=True)
        heads.append(_dot(p.astype(BF16), vm_ref[0, :, lanes]) / denom)
    om = jnp.concatenate(heads, axis=-1).astype(BF16)
    ob = jnp.concatenate([ob_ref[0, hp] for hp in range(B_PAIRS)], axis=-1)
    mixed = None
    for i, branch in enumerate((oa_ref[0], ob, om)):
        gate = jax.nn.sigmoid(_dot(xb, wg_ref[:, i * D_MODEL:(i + 1) * D_MODEL]))
        term = gate * _dot(branch, wbr_ref[i])
        mixed = term if mixed is None else mixed + term
    y = ALPHA * x + _dot(mixed.astype(BF16), wo_ref[...])
    o_ref[0] = _layer_norm(y, g_ref[...], b_ref[...])


def _merge_ln(x, oa, ob, km, vm, w_qm, w_gate, w_branch, w_out, g, b):
    bsz, seq, _ = x.shape
    rows = MERGE_ROWS
    tok = lambda cols: pl.BlockSpec((1, rows, cols), lambda bi, i: (bi, i, 0))
    mem = pl.BlockSpec((1, N_MEM, QM_COLS), lambda bi, i: (bi, 0, 0))
    return pl.pallas_call(
        _merge_ln_kernel,
        grid=(bsz, seq // rows),
        in_specs=[tok(D_MODEL), tok(QA_COLS),
                  pl.BlockSpec((1, B_PAIRS, rows, LANES), lambda bi, i: (bi, 0, i, 0)),
                  mem, mem,
                  _resident(w_qm.shape), _resident(w_gate.shape), _resident(w_branch.shape),
                  _resident(w_out.shape), _resident((1, D_MODEL)), _resident((1, D_MODEL))],
        out_specs=tok(D_MODEL),
        out_shape=jax.ShapeDtypeStruct((bsz, seq, D_MODEL), F32),
        compiler_params=_params(2),
        name="merge_ln",
    )(x, oa, ob, km, vm, w_qm, w_gate, w_branch, w_out, g, b)


def _slab_order():
    cols = []
    for i in range(A_GROUP):
        cols += list(range(i * HEAD_DIM, (i + 1) * HEAD_DIM))
        cols += list(range((A_GROUP + i) * HEAD_DIM, (A_GROUP + i + 1) * HEAD_DIM))
    return jnp.asarray(cols, dtype=jnp.int32)


def kernel(x_prompt, x_sample, mem_prompt, mem_sample, ffn1_w_in, ffn1_w_out, ln1_g, ln1_b, w_in, w_mem_kv,
           sink_a, w_branch, w_out, ln2_g, ln2_b, ffn2_w_in, ffn2_w_out, ln3_g, ln3_b):
    order = _slab_order()
    bias_a = _attn_a_bias()
    biases_b = _attn_b_biases()
    q_scale = HEAD_DIM ** -0.5

    layers = []
    for l in range(DEPTH):
        w = w_in[l]
        qa_w = w[:, :QA_COLS][:, order] * q_scale
        kva_w = w[:, QA_COLS:QA_COLS + KVA_COLS]
        qb_w = w[:, QA_COLS + KVA_COLS:QA_COLS + KVA_COLS + BRANCH_WIDTH] * q_scale
        kvb_w = w[:, QA_COLS + KVA_COLS + BRANCH_WIDTH:QM_START]
        w_proj = jnp.concatenate([qa_w, kva_w, qb_w, kvb_w], axis=1).astype(BF16)
        w_br = jnp.concatenate([w_branch[l, :1][:, order], w_branch[l, 1:]], axis=0).astype(BF16)
        sink_col = jnp.repeat(sink_a[l].astype(F32), A_BLK).reshape(A_KV_HEADS, A_GROUP * A_BLK, 1)
        row = lambda t: t[l].reshape(1, D_MODEL).astype(F32)
        layers.append(dict(
            ffn1=(ffn1_w_in[l].astype(BF16), ffn1_w_out[l].astype(BF16), row(ln1_g), row(ln1_b)),
            ffn2=(ffn2_w_in[l].astype(BF16), ffn2_w_out[l].astype(BF16), row(ln3_g), row(ln3_b)),
            w_proj=w_proj,
            w_mem=w_mem_kv[l].astype(BF16),
            sink_col=sink_col,
            merge=(w[:, QM_START:GATE_START].astype(BF16), w[:, GATE_START:].astype(BF16), w_br,
                   w_out[l].astype(BF16), row(ln2_g), row(ln2_b)),
        ))

    def trunk(x, mem):
        bsz, seq, _ = x.shape
        for p in layers:
            x = _ffn_ln(x.reshape(bsz * seq, D_MODEL), *p["ffn1"]).reshape(bsz, seq, D_MODEL)
            qa, ka, va, qb, kb, vb = _in_proj(x, p["w_proj"])
            km, vm = _mem_kv(mem, p["w_mem"])
            oa = _attn_a(qa, ka, va, p["sink_col"], bias_a)
            ob = _attn_b(qb, kb, vb, biases_b)
            x = _merge_ln(x, oa, ob, km, vm, *p["merge"])
            x = _ffn_ln(x.reshape(bsz * seq, D_MODEL), *p["ffn2"]).reshape(bsz, seq, D_MODEL)
        return x

    return trunk(x_prompt, mem_prompt), trunk(x_sample, mem_sample)
```

```python
import functools

import jax
import jax.numpy as jnp
from jax import lax
from jax.experimental import pallas as pl
from jax.experimental.pallas import tpu as pltpu

F32 = jnp.float32
BF16 = jnp.bfloat16

D_MODEL = 1024
DEPTH = 2
HEAD_DIM = 64
A_HEADS = 8
A_KV_HEADS = 2
A_GROUP = A_HEADS // A_KV_HEADS
A_RADIUS = 128
B_HEADS = 8
B_RADIUS = 64
M_HEADS = 4
M_HEAD_DIM = 128
N_MEM = 256
N_BRANCH = 3
BRANCH_WIDTH = 512
D_FF = 2816
LN_EPS = 1e-5
ALPHA = (2 * DEPTH) ** 0.25
NEG_INF = -1e30

QA_COLS = A_HEADS * HEAD_DIM
KVA_COLS = 2 * A_KV_HEADS * HEAD_DIM
QKVB_COLS = 3 * B_HEADS * HEAD_DIM
QM_COLS = M_HEADS * M_HEAD_DIM
GATE_COLS = N_BRANCH * D_MODEL
QM_START = QA_COLS + KVA_COLS + QKVB_COLS
GATE_START = QM_START + QM_COLS

LANES = 128
MAX_DIL = 16
B_PAIRS = B_HEADS * HEAD_DIM // LANES
A_SLABS = QA_COLS // LANES
PROJ_COLS = QA_COLS + KVA_COLS + QKVB_COLS

FFN_ROWS = 512
FFN_CHUNK = 256
PROJ_ROWS = 512
MERGE_ROWS = 512
A_TILE = 1024
A_BLK = A_RADIUS
A_KEYS = 3 * A_BLK
B_TILE = 128
B_KEYS = 2 * B_TILE
SUPER = B_TILE * MAX_DIL

VMEM_LIMIT = 56 * 1024 * 1024


def _resident(shape):
    zeros = (0,) * len(shape)
    return pl.BlockSpec(shape, lambda *_: zeros, pipeline_mode=pl.Buffered(1))


def _params(n_axes):
    return pltpu.CompilerParams(dimension_semantics=("parallel",) * n_axes,
                                vmem_limit_bytes=VMEM_LIMIT)


def _layer_norm(y, g, b):
    mu = jnp.mean(y, axis=-1, keepdims=True)
    d = y - mu
    var = jnp.mean(d * d, axis=-1, keepdims=True)
    return d * lax.rsqrt(var + LN_EPS) * g + b


def _dot(a, b):
    return jnp.dot(a, b, preferred_element_type=F32)


def _dot_nt(a, b):
    return lax.dot_general(a, b, (((1,), (1,)), ((), ())), preferred_element_type=F32)


def _alibi_slopes(n):
    return 2.0 ** (-8.0 * jnp.arange(1, n + 1, dtype=F32) / n)


def _ffn_ln_kernel(x_ref, win_ref, wout_ref, g_ref, b_ref, o_ref):
    x = x_ref[...]
    xb = x.astype(BF16)
    acc = None
    for c in range(D_FF // FFN_CHUNK):
        lo = c * FFN_CHUNK
        gate = _dot(xb, win_ref[:, lo:lo + FFN_CHUNK])
        up = _dot(xb, win_ref[:, D_FF + lo:D_FF + lo + FFN_CHUNK])
        act = (gate * jax.nn.sigmoid(gate) * up).astype(BF16)
        part = _dot(act, wout_ref[lo:lo + FFN_CHUNK, :])
        acc = part if acc is None else acc + part
    o_ref[...] = _layer_norm(ALPHA * x + 0.5 * acc, g_ref[...], b_ref[...])


def _ffn_ln(x2d, w_in, w_out, g, b):
    rows = x2d.shape[0]
    return pl.pallas_call(
        _ffn_ln_kernel,
        grid=(rows // FFN_ROWS,),
        in_specs=[
            pl.BlockSpec((FFN_ROWS, D_MODEL), lambda i: (i, 0)),
            _resident((D_MODEL, 2 * D_FF)),
            _resident((D_FF, D_MODEL)),
            _resident((1, D_MODEL)),
            _resident((1, D_MODEL)),
        ],
        out_specs=pl.BlockSpec((FFN_ROWS, D_MODEL), lambda i: (i, 0)),
        out_shape=jax.ShapeDtypeStruct((rows, D_MODEL), F32),
        compiler_params=_params(1),
        name="ffn_ln",
    )(x2d, w_in, w_out, g, b)


def _in_proj_kernel(x_ref, w_ref, qa_ref, ka_ref, va_ref, qb_ref, kb_ref, vb_ref):
    xb = x_ref[0].astype(BF16)
    qa_ref[0] = _dot(xb, w_ref[:, 0:QA_COLS]).astype(BF16)
    kva = _dot(xb, w_ref[:, QA_COLS:QA_COLS + KVA_COLS]).astype(BF16)
    ka_ref[0] = kva[:, :LANES]
    va_ref[0] = kva[:, LANES:]
    base = QA_COLS + KVA_COLS
    for n, ref in enumerate((qb_ref, kb_ref, vb_ref)):
        lo = base + n * BRANCH_WIDTH
        res = _dot(xb, w_ref[:, lo:lo + BRANCH_WIDTH]).astype(BF16)
        for hp in range(B_PAIRS):
            ref[0, hp] = res[:, hp * LANES:(hp + 1) * LANES]


def _in_proj(x, w):
    bsz, seq, _ = x.shape
    rows = PROJ_ROWS
    tok = lambda cols: pl.BlockSpec((1, rows, cols), lambda b, i: (b, i, 0))
    pair = pl.BlockSpec((1, B_PAIRS, rows, LANES), lambda b, i: (b, 0, i, 0))
    pair_shape = jax.ShapeDtypeStruct((bsz, B_PAIRS, seq, LANES), BF16)
    return pl.pallas_call(
        _in_proj_kernel,
        grid=(bsz, seq // rows),
        in_specs=[tok(D_MODEL), _resident((D_MODEL, PROJ_COLS))],
        out_specs=[tok(QA_COLS), tok(LANES), tok(LANES), pair, pair, pair],
        out_shape=[
            jax.ShapeDtypeStruct((bsz, seq, QA_COLS), BF16),
            jax.ShapeDtypeStruct((bsz, seq, LANES), BF16),
            jax.ShapeDtypeStruct((bsz, seq, LANES), BF16),
            pair_shape, pair_shape, pair_shape,
        ],
        compiler_params=_params(2),
        name="in_proj",
    )(x, w)


def _mem_kv_kernel(mem_ref, w_ref, km_ref, vm_ref):
    res = _dot(mem_ref[0].astype(BF16), w_ref[...]).astype(BF16)
    km_ref[0] = res[:, :QM_COLS]
    vm_ref[0] = res[:, QM_COLS:]


def _mem_kv(mem, w):
    bsz = mem.shape[0]
    out = pl.BlockSpec((1, N_MEM, QM_COLS), lambda b: (b, 0, 0))
    shape = jax.ShapeDtypeStruct((bsz, N_MEM, QM_COLS), BF16)
    return pl.pallas_call(
        _mem_kv_kernel,
        grid=(bsz,),
        in_specs=[pl.BlockSpec((1, N_MEM, D_MODEL), lambda b: (b, 0, 0)),
                  _resident((D_MODEL, 2 * QM_COLS))],
        out_specs=[out, out],
        out_shape=[shape, shape],
        compiler_params=_params(1),
        name="mem_kv",
    )(mem, w)


def _edge_variant(start, last_start):
    return jnp.where(start == 0, 0, jnp.where(start == last_start, 2, 1))


def _band_bias(slopes, q_pos, k_pos, shifts, radius, dil):
    tables = []
    for shift in shifts:
        rel = k_pos[None, :] - shift - q_pos[:, None]
        dist = jnp.abs(rel).astype(F32) * dil
        table = jnp.where((jnp.abs(rel) <= radius)[None], -slopes[:, None, None] * dist[None], NEG_INF)
        tables.append(table.reshape(-1, k_pos.shape[0]))
    return jnp.stack(tables)


def _attn_a_kernel(sink_ref, q_ref, k_ref, v_ref, bias_ref, o_ref, *, seq):
    tile = pl.program_id(1)
    low_half = lax.broadcasted_iota(jnp.int32, (1, LANES), 1) < HEAD_DIM
    zero = jnp.zeros((), BF16)

    def block(j, carry):
        row = pl.multiple_of(j * A_BLK, A_BLK)
        start = tile * A_TILE + row
        window = pl.multiple_of(jnp.clip(start - A_BLK, 0, seq - A_KEYS), A_BLK)
        q4 = q_ref[0, pl.ds(row, A_BLK), :]
        k3 = k_ref[0, pl.ds(window, A_KEYS), :]
        v3 = v_ref[0, pl.ds(window, A_KEYS), :]
        lhs = jnp.concatenate(
            [jnp.where(low_half if g == 0 else jnp.logical_not(low_half),
                       q4[:, i * LANES:(i + 1) * LANES], zero)
             for g in range(A_KV_HEADS) for i in range(A_SLABS)], axis=0)
        s = _dot_nt(lhs, k3) + bias_ref[_edge_variant(start, seq - A_BLK)]
        probs, totals = [], []
        for h in range(A_HEADS):
            sh = s[h * A_BLK:(h + 1) * A_BLK]
            sink = sink_ref[h]
            m = jnp.maximum(jnp.max(sh, axis=-1, keepdims=True), sink)
            p = jnp.exp(sh - m)
            totals.append(jnp.sum(p, axis=-1, keepdims=True) + jnp.exp(sink - m))
            probs.append(p.astype(BF16))
        o = _dot(jnp.concatenate(probs, axis=0), v3)
        for i in range(A_SLABS):
            lo, hi = i, A_GROUP + i
            num = jnp.where(low_half, o[lo * A_BLK:(lo + 1) * A_BLK], o[hi * A_BLK:(hi + 1) * A_BLK])
            den = jnp.where(low_half, totals[lo], totals[hi])
            o_ref[0, pl.ds(row, A_BLK), i * LANES:(i + 1) * LANES] = (num / den).astype(BF16)
        return carry

    lax.fori_loop(0, A_TILE // A_BLK, block, 0)


def _attn_a(qa, ka, va, sink, bias):
    bsz, seq, _ = qa.shape
    qspec = pl.BlockSpec((1, A_TILE, QA_COLS), lambda b, i: (b, i, 0))
    kvspec = pl.BlockSpec((1, seq, LANES), lambda b, i: (b, 0, 0))
    return pl.pallas_call(
        functools.partial(_attn_a_kernel, seq=seq),
        grid=(bsz, seq // A_TILE),
        in_specs=[pl.BlockSpec(memory_space=pltpu.SMEM), qspec, kvspec, kvspec, _resident(bias.shape)],
        out_specs=qspec,
        out_shape=jax.ShapeDtypeStruct((bsz, seq, QA_COLS), BF16),
        compiler_params=_params(2),
        name="attn_a",
    )(sink, qa, ka, va, bias)


def _attn_a_bias():
    ar = jnp.arange
    return _band_bias(_alibi_slopes(A_HEADS), ar(A_BLK), ar(A_KEYS), (0, A_BLK, 2 * A_BLK), A_RADIUS, 1)


def _pair_attend(q2, k2, v2, bias, low_half):
    n = q2.shape[0]
    zero = jnp.zeros((), BF16)
    lhs = jnp.concatenate([jnp.where(low_half, q2, zero), jnp.where(low_half, zero, q2)], axis=0)
    s = _dot_nt(lhs, k2) + bias
    m = jnp.max(s, axis=-1, keepdims=True)
    p = jnp.exp(s - m)
    denom = jnp.sum(p, axis=-1, keepdims=True)
    o = _dot(p.astype(BF16), v2)
    denom = jnp.where(low_half, denom[:n], denom[n:])
    out = jnp.where(low_half, o[:n], o[n:]) / denom
    lse = jnp.where(low_half, m[:n], m[n:]) + jnp.log(denom)
    return out, lse


def _attn_b_kernel(q_ref, k_ref, v_ref, b1_ref, b4_ref, b16_ref, o_ref, nat32, qs, ks, vs, osc, lsc,
                   *, seq):
    rows = seq // MAX_DIL
    for src, dst in ((q_ref, qs), (k_ref, ks), (v_ref, vs)):
        nat32[...] = src[0, 0].astype(F32)
        for r in range(MAX_DIL):
            dst[r] = nat32[pl.ds(r, rows, stride=MAX_DIL), :].astype(BF16)
    low_half = lax.broadcasted_iota(jnp.int32, (1, LANES), 1) < HEAD_DIM

    def super_tile(st, carry):
        t0 = pl.multiple_of(st * SUPER, SUPER)
        u0 = pl.multiple_of(st * B_TILE, B_TILE)

        whole = rows == B_TILE
        keys16 = B_TILE if whole else B_KEYS
        win16 = 0 if whole else pl.multiple_of(jnp.clip(u0 - B_RADIUS, 0, rows - B_KEYS), B_RADIUS)

        def d16(r, c):
            q2 = qs[r, pl.ds(u0, B_TILE), :]
            k2 = ks[r, pl.ds(win16, keys16), :]
            v2 = vs[r, pl.ds(win16, keys16), :]
            bias = b16_ref[0, 0 if whole else _edge_variant(u0, rows - B_TILE)]
            out, lse = _pair_attend(q2, k2, v2, bias, low_half)
            osc[2, pl.ds(r, B_TILE, stride=MAX_DIL), :] = out
            lsc[2, pl.ds(r, B_TILE, stride=MAX_DIL), :] = lse
            return c

        lax.fori_loop(0, MAX_DIL, d16, 0, unroll=4)

        def d4(n, c):
            r4 = n // 4
            part = n % 4
            u1 = pl.multiple_of(u0 + part * 32, 32)
            win = pl.multiple_of(jnp.clip(u1 - 16, 0, rows - 64), 16)
            q2 = jnp.concatenate([qs[4 * a + r4, pl.ds(u1, 32), :] for a in range(4)], axis=0)
            k2 = jnp.concatenate([ks[4 * a + r4, pl.ds(win, 64), :] for a in range(4)], axis=0)
            v2 = jnp.concatenate([vs[4 * a + r4, pl.ds(win, 64), :] for a in range(4)], axis=0)
            out, lse = _pair_attend(q2, k2, v2, b4_ref[0, _edge_variant(u1, rows - 32)], low_half)
            for a in range(4):
                dst = pl.ds(part * (32 * MAX_DIL) + 4 * a + r4, 32, stride=MAX_DIL)
                osc[1, dst, :] = out[a * 32:(a + 1) * 32]
                lsc[1, dst, :] = lse[a * 32:(a + 1) * 32]
            return c

        lax.fori_loop(0, 16, d4, 0, unroll=4)

        def d1(n, c):
            off = pl.multiple_of(n * B_TILE, B_TILE)
            start = t0 + off
            win = pl.multiple_of(jnp.clip(start - B_RADIUS, 0, seq - B_KEYS), B_RADIUS)
            q2 = q_ref[0, 0, pl.ds(start, B_TILE), :]
            k2 = k_ref[0, 0, pl.ds(win, B_KEYS), :]
            v2 = v_ref[0, 0, pl.ds(win, B_KEYS), :]
            out, lse = _pair_attend(q2, k2, v2, b1_ref[0, _edge_variant(start, seq - B_TILE)], low_half)
            osc[0, pl.ds(off, B_TILE), :] = out
            lsc[0, pl.ds(off, B_TILE), :] = lse
            return c

        lax.fori_loop(0, SUPER // B_TILE, d1, 0, unroll=4)

        l0, l1, l2 = lsc[0], lsc[1], lsc[2]
        m = jnp.maximum(jnp.maximum(l0, l1), l2)
        e0, e1, e2 = jnp.exp(l0 - m), jnp.exp(l1 - m), jnp.exp(l2 - m)
        mixed = (e0 * osc[0] + e1 * osc[1] + e2 * osc[2]) / (e0 + e1 + e2)
        o_ref[0, 0, pl.ds(t0, SUPER), :] = mixed.astype(BF16)
        return carry

    lax.fori_loop(0, seq // SUPER, super_tile, 0)


def _attn_b(qb, kb, vb, biases):
    bsz, _, seq, _ = qb.shape
    rows = seq // MAX_DIL
    spec = pl.BlockSpec((1, 1, seq, LANES), lambda hp, b: (b, hp, 0, 0))
    bias_specs = [pl.BlockSpec((1,) + t.shape[1:], lambda hp, b: (hp, 0, 0, 0)) for t in biases]
    return pl.pallas_call(
        functools.partial(_attn_b_kernel, seq=seq),
        grid=(B_PAIRS, bsz),
        in_specs=[spec, spec, spec] + bias_specs,
        out_specs=spec,
        out_shape=jax.ShapeDtypeStruct((bsz, B_PAIRS, seq, LANES), BF16),
        scratch_shapes=[
            pltpu.VMEM((seq, LANES), F32),
            pltpu.VMEM((MAX_DIL, rows, LANES), BF16),
            pltpu.VMEM((MAX_DIL, rows, LANES), BF16),
            pltpu.VMEM((MAX_DIL, rows, LANES), BF16),
            pltpu.VMEM((3, SUPER, LANES), F32),
            pltpu.VMEM((3, SUPER, LANES), F32),
        ],
        compiler_params=_params(2),
        name="attn_b",
    )(qb, kb, vb, *biases)


def _attn_b_biases(seq):
    slopes = _alibi_slopes(B_HEADS)
    ar = jnp.arange
    pairs = lambda make: jnp.stack([make(slopes[2 * hp:2 * hp + 2]) for hp in range(B_PAIRS)])
    b1 = pairs(lambda sl: _band_bias(sl, ar(B_TILE), ar(B_KEYS), (0, B_RADIUS, 2 * B_RADIUS), B_RADIUS, 1))
    a4 = ar(4)[:, None]
    q4 = (4 * ar(32)[None, :] + a4).reshape(-1)
    k4 = (4 * ar(64)[None, :] + a4).reshape(-1)
    b4 = pairs(lambda sl: _band_bias(sl, q4, k4, (0, 64, 128), B_RADIUS, 4))
    if seq // MAX_DIL == B_TILE:
        b16 = pairs(lambda sl: _band_bias(sl, ar(B_TILE), ar(B_TILE), (0,), B_RADIUS, 16))
    else:
        b16 = pairs(lambda sl: _band_bias(sl, ar(B_TILE), ar(B_KEYS), (0, B_RADIUS, 2 * B_RADIUS), B_RADIUS, 16))
    return b1, b4, b16


def _merge_ln_kernel(x_ref, oa_ref, ob_ref, km_ref, vm_ref, wqm_ref, wg_ref, wbr_ref, wo_ref,
                     g_ref, b_ref, o_ref):
    x = x_ref[0]
    xb = x.astype(BF16)
    qm = _dot(xb, wqm_ref[...]).astype(BF16)
    heads = []
    for h in range(M_HEADS):
        lanes = slice(h * M_HEAD_DIM, (h + 1) * M_HEAD_DIM)
        s = _dot_nt(qm[:, lanes], km_ref[0, :, lanes]) * (M_HEAD_DIM ** -0.5)
        p = jnp.exp(s - jnp.max(s, axis=-1, keepdims=True))
        denom = jnp.sum(p, axis=-1, keepdims=True)
        heads.append(_dot(p.astype(BF16), vm_ref[0, :, lanes]) / denom)
    om = jnp.concatenate(heads, axis=-1).astype(BF16)
    ob = jnp.concatenate([ob_ref[0, hp] for hp in range(B_PAIRS)], axis=-1)
    mixed = None
    for i, branch in enumerate((oa_ref[0], ob, om)):
        gate = jax.nn.sigmoid(_dot(xb, wg_ref[:, i * D_MODEL:(i + 1) * D_MODEL]))
        term = gate * _dot(branch, wbr_ref[i])
        mixed = term if mixed is None else mixed + term
    y = ALPHA * x + _dot(mixed.astype(BF16), wo_ref[...])
    o_ref[0] = _layer_norm(y, g_ref[...], b_ref[...])


def _merge_ln(x, oa, ob, km, vm, w_qm, w_gate, w_branch, w_out, g, b):
    bsz, seq, _ = x.shape
    rows = MERGE_ROWS
    tok = lambda cols: pl.BlockSpec((1, rows, cols), lambda bi, i: (bi, i, 0))
    mem = pl.BlockSpec((1, N_MEM, QM_COLS), lambda bi, i: (bi, 0, 0))
    return pl.pallas_call(
        _merge_ln_kernel,
        grid=(bsz, seq // rows),
        in_specs=[tok(D_MODEL), tok(QA_COLS),
                  pl.BlockSpec((1, B_PAIRS, rows, LANES), lambda bi, i: (bi, 0, i, 0)),
                  mem, mem,
                  _resident(w_qm.shape), _resident(w_gate.shape), _resident(w_branch.shape),
                  _resident(w_out.shape), _resident((1, D_MODEL)), _resident((1, D_MODEL))],
        out_specs=tok(D_MODEL),
        out_shape=jax.ShapeDtypeStruct((bsz, seq, D_MODEL), F32),
        compiler_params=_params(2),
        name="merge_ln",
    )(x, oa, ob, km, vm, w_qm, w_gate, w_branch, w_out, g, b)


def _slab_order():
    cols = []
    for i in range(A_GROUP):
        cols += list(range(i * HEAD_DIM, (i + 1) * HEAD_DIM))
        cols += list(range((A_GROUP + i) * HEAD_DIM, (A_GROUP + i + 1) * HEAD_DIM))
    return jnp.asarray(cols, dtype=jnp.int32)


def kernel(x_prompt, x_sample, mem_prompt, mem_sample, ffn1_w_in, ffn1_w_out, ln1_g, ln1_b, w_in, w_mem_kv,
           sink_a, w_branch, w_out, ln2_g, ln2_b, ffn2_w_in, ffn2_w_out, ln3_g, ln3_b):
    order = _slab_order()
    bias_a = _attn_a_bias()
    q_scale = HEAD_DIM ** -0.5

    layers = []
    for l in range(DEPTH):
        w = w_in[l]
        qa_w = w[:, :QA_COLS][:, order] * q_scale
        kva_w = w[:, QA_COLS:QA_COLS + KVA_COLS]
        qb_w = w[:, QA_COLS + KVA_COLS:QA_COLS + KVA_COLS + BRANCH_WIDTH] * q_scale
        kvb_w = w[:, QA_COLS + KVA_COLS + BRANCH_WIDTH:QM_START]
        w_proj = jnp.concatenate([qa_w, kva_w, qb_w, kvb_w], axis=1).astype(BF16)
        w_br = jnp.concatenate([w_branch[l, :1][:, order], w_branch[l, 1:]], axis=0).astype(BF16)
        row = lambda t: t[l].reshape(1, D_MODEL).astype(F32)
        layers.append(dict(
            ffn1=(ffn1_w_in[l].astype(BF16), ffn1_w_out[l].astype(BF16), row(ln1_g), row(ln1_b)),
            ffn2=(ffn2_w_in[l].astype(BF16), ffn2_w_out[l].astype(BF16), row(ln3_g), row(ln3_b)),
            w_proj=w_proj,
            w_mem=w_mem_kv[l].astype(BF16),
            sink=sink_a[l].astype(F32),
            merge=(w[:, QM_START:GATE_START].astype(BF16), w[:, GATE_START:].astype(BF16), w_br,
                   w_out[l].astype(BF16), row(ln2_g), row(ln2_b)),
        ))

    def trunk(x, mem):
        bsz, seq, _ = x.shape
        biases_b = _attn_b_biases(seq)
        for p in layers:
            x = _ffn_ln(x.reshape(bsz * seq, D_MODEL), *p["ffn1"]).reshape(bsz, seq, D_MODEL)
            qa, ka, va, qb, kb, vb = _in_proj(x, p["w_proj"])
            km, vm = _mem_kv(mem, p["w_mem"])
            oa = _attn_a(qa, ka, va, p["sink"], bias_a)
            ob = _attn_b(qb, kb, vb, biases_b)
            x = _merge_ln(x, oa, ob, km, vm, *p["merge"])
            x = _ffn_ln(x.reshape(bsz * seq, D_MODEL), *p["ffn2"]).reshape(bsz, seq, D_MODEL)
        return x

    return trunk(x_prompt, mem_prompt), trunk(x_sample, mem_sample)
```

```python
import functools

import jax
import jax.numpy as jnp
from jax import lax
from jax.experimental import pallas as pl
from jax.experimental.pallas import tpu as pltpu

F32 = jnp.float32
BF16 = jnp.bfloat16

D_MODEL = 1024
DEPTH = 2
HEAD_DIM = 64
A_HEADS = 8
A_KV_HEADS = 2
A_GROUP = A_HEADS // A_KV_HEADS
A_RADIUS = 128
B_HEADS = 8
B_RADIUS = 64
M_HEADS = 4
M_HEAD_DIM = 128
N_MEM = 256
N_BRANCH = 3
BRANCH_WIDTH = 512
D_FF = 2816
LN_EPS = 1e-5
ALPHA = (2 * DEPTH) ** 0.25
NEG_INF = -1e30

QA_COLS = A_HEADS * HEAD_DIM
KVA_COLS = 2 * A_KV_HEADS * HEAD_DIM
QKVB_COLS = 3 * B_HEADS * HEAD_DIM
QM_COLS = M_HEADS * M_HEAD_DIM
GATE_COLS = N_BRANCH * D_MODEL
QM_START = QA_COLS + KVA_COLS + QKVB_COLS
GATE_START = QM_START + QM_COLS

LANES = 128
MAX_DIL = 16
B_PAIRS = B_HEADS * HEAD_DIM // LANES
A_SLABS = QA_COLS // LANES
PROJ_COLS = QA_COLS + KVA_COLS + QKVB_COLS

FFN_ROWS = 512
FFN_CHUNK = 256
PROJ_ROWS = 512
MERGE_ROWS = 512
A_TILE = 1024
A_BLK = A_RADIUS
A_KEYS = 3 * A_BLK
B_TILE = 128
B_KEYS = 2 * B_TILE
SUPER = B_TILE * MAX_DIL

VMEM_LIMIT = 56 * 1024 * 1024


def _resident(shape):
    zeros = (0,) * len(shape)
    return pl.BlockSpec(shape, lambda *_: zeros, pipeline_mode=pl.Buffered(1))


def _params(n_axes):
    return pltpu.CompilerParams(dimension_semantics=("parallel",) * n_axes,
                                vmem_limit_bytes=VMEM_LIMIT)


def _layer_norm(y, g, b):
    mu = jnp.mean(y, axis=-1, keepdims=True)
    d = y - mu
    var = jnp.mean(d * d, axis=-1, keepdims=True)
    return d * lax.rsqrt(var + LN_EPS) * g + b


def _dot(a, b):
    return jnp.dot(a, b, preferred_element_type=F32)


def _dot_nt(a, b):
    return lax.dot_general(a, b, (((1,), (1,)), ((), ())), preferred_element_type=F32)


def _alibi_slopes(n):
    return 2.0 ** (-8.0 * jnp.arange(1, n + 1, dtype=F32) / n)


def _ffn_ln_kernel(x_ref, win_ref, wout_ref, g_ref, b_ref, o_ref):
    x = x_ref[...]
    xb = x.astype(BF16)
    acc = None
    for c in range(D_FF // FFN_CHUNK):
        lo = c * FFN_CHUNK
        gate = _dot(xb, win_ref[:, lo:lo + FFN_CHUNK])
        up = _dot(xb, win_ref[:, D_FF + lo:D_FF + lo + FFN_CHUNK])
        act = (gate * jax.nn.sigmoid(gate) * up).astype(BF16)
        part = _dot(act, wout_ref[lo:lo + FFN_CHUNK, :])
        acc = part if acc is None else acc + part
    o_ref[...] = _layer_norm(ALPHA * x + 0.5 * acc, g_ref[...], b_ref[...])


def _ffn_ln(x2d, w_in, w_out, g, b):
    rows = x2d.shape[0]
    return pl.pallas_call(
        _ffn_ln_kernel,
        grid=(rows // FFN_ROWS,),
        in_specs=[
            pl.BlockSpec((FFN_ROWS, D_MODEL), lambda i: (i, 0)),
            _resident((D_MODEL, 2 * D_FF)),
            _resident((D_FF, D_MODEL)),
            _resident((1, D_MODEL)),
            _resident((1, D_MODEL)),
        ],
        out_specs=pl.BlockSpec((FFN_ROWS, D_MODEL), lambda i: (i, 0)),
        out_shape=jax.ShapeDtypeStruct((rows, D_MODEL), F32),
        compiler_params=_params(1),
        name="ffn_ln",
    )(x2d, w_in, w_out, g, b)


def _in_proj_kernel(x_ref, w_ref, qa_ref, ka_ref, va_ref, qb_ref, kb_ref, vb_ref,
                    qr_ref, kr_ref, vr_ref, stage):
    xb = x_ref[0].astype(BF16)
    qa_ref[0] = _dot(xb, w_ref[:, 0:QA_COLS]).astype(BF16)
    kva = _dot(xb, w_ref[:, QA_COLS:QA_COLS + KVA_COLS]).astype(BF16)
    ka_ref[0] = kva[:, :LANES]
    va_ref[0] = kva[:, LANES:]
    base = QA_COLS + KVA_COLS
    sub = PROJ_ROWS // MAX_DIL
    for n, (ref, res_ref) in enumerate(((qb_ref, qr_ref), (kb_ref, kr_ref), (vb_ref, vr_ref))):
        lo = base + n * BRANCH_WIDTH
        res = _dot(xb, w_ref[:, lo:lo + BRANCH_WIDTH])
        for hp in range(B_PAIRS):
            slab = res[:, hp * LANES:(hp + 1) * LANES]
            ref[0, hp] = slab.astype(BF16)
            stage[n, hp] = slab
            for r in range(MAX_DIL):
                res_ref[0, hp, r] = stage[n, hp, pl.ds(r, sub, stride=MAX_DIL), :].astype(BF16)


def _in_proj(x, w):
    bsz, seq, _ = x.shape
    rows = PROJ_ROWS
    sub = rows // MAX_DIL
    tok = lambda cols: pl.BlockSpec((1, rows, cols), lambda b, i: (b, i, 0))
    pair = pl.BlockSpec((1, B_PAIRS, rows, LANES), lambda b, i: (b, 0, i, 0))
    pair_shape = jax.ShapeDtypeStruct((bsz, B_PAIRS, seq, LANES), BF16)
    resid = pl.BlockSpec((1, B_PAIRS, MAX_DIL, sub, LANES), lambda b, i: (b, 0, 0, i, 0))
    resid_shape = jax.ShapeDtypeStruct((bsz, B_PAIRS, MAX_DIL, seq // MAX_DIL, LANES), BF16)
    return pl.pallas_call(
        _in_proj_kernel,
        grid=(bsz, seq // rows),
        in_specs=[tok(D_MODEL), _resident((D_MODEL, PROJ_COLS))],
        out_specs=[tok(QA_COLS), tok(LANES), tok(LANES), pair, pair, pair, resid, resid, resid],
        out_shape=[
            jax.ShapeDtypeStruct((bsz, seq, QA_COLS), BF16),
            jax.ShapeDtypeStruct((bsz, seq, LANES), BF16),
            jax.ShapeDtypeStruct((bsz, seq, LANES), BF16),
            pair_shape, pair_shape, pair_shape, resid_shape, resid_shape, resid_shape,
        ],
        scratch_shapes=[pltpu.VMEM((3, B_PAIRS, rows, LANES), F32)],
        compiler_params=_params(2),
        name="in_proj",
    )(x, w)


def _mem_kv_kernel(mem_ref, w_ref, km_ref, vm_ref):
    res = _dot(mem_ref[0].astype(BF16), w_ref[...]).astype(BF16)
    km_ref[0] = res[:, :QM_COLS]
    vm_ref[0] = res[:, QM_COLS:]


def _mem_kv(mem, w):
    bsz = mem.shape[0]
    out = pl.BlockSpec((1, N_MEM, QM_COLS), lambda b: (b, 0, 0))
    shape = jax.ShapeDtypeStruct((bsz, N_MEM, QM_COLS), BF16)
    return pl.pallas_call(
        _mem_kv_kernel,
        grid=(bsz,),
        in_specs=[pl.BlockSpec((1, N_MEM, D_MODEL), lambda b: (b, 0, 0)),
                  _resident((D_MODEL, 2 * QM_COLS))],
        out_specs=[out, out],
        out_shape=[shape, shape],
        compiler_params=_params(1),
        name="mem_kv",
    )(mem, w)


def _edge_variant(start, last_start):
    return jnp.where(start == 0, 0, jnp.where(start == last_start, 2, 1))


def _band_bias(slopes, q_pos, k_pos, shifts, radius, dil):
    tables = []
    for shift in shifts:
        rel = k_pos[None, :] - shift - q_pos[:, None]
        dist = jnp.abs(rel).astype(F32) * dil
        table = jnp.where((jnp.abs(rel) <= radius)[None], -slopes[:, None, None] * dist[None], NEG_INF)
        tables.append(table.reshape(-1, k_pos.shape[0]))
    return jnp.stack(tables)


def _attn_a_kernel(sink_ref, q_ref, k_ref, v_ref, bias_ref, o_ref, *, seq):
    tile = pl.program_id(1)
    low_half = lax.broadcasted_iota(jnp.int32, (1, LANES), 1) < HEAD_DIM
    zero = jnp.zeros((), BF16)

    def block(j, carry):
        row = pl.multiple_of(j * A_BLK, A_BLK)
        start = tile * A_TILE + row
        window = pl.multiple_of(jnp.clip(start - A_BLK, 0, seq - A_KEYS), A_BLK)
        q4 = q_ref[0, pl.ds(row, A_BLK), :]
        k3 = k_ref[0, pl.ds(window, A_KEYS), :]
        v3 = v_ref[0, pl.ds(window, A_KEYS), :]
        lhs = jnp.concatenate(
            [jnp.where(low_half if g == 0 else jnp.logical_not(low_half),
                       q4[:, i * LANES:(i + 1) * LANES], zero)
             for g in range(A_KV_HEADS) for i in range(A_SLABS)], axis=0)
        s = _dot_nt(lhs, k3) + bias_ref[_edge_variant(start, seq - A_BLK)]
        probs, totals = [], []
        for h in range(A_HEADS):
            sh = s[h * A_BLK:(h + 1) * A_BLK]
            sink = sink_ref[h]
            m = jnp.maximum(jnp.max(sh, axis=-1, keepdims=True), sink)
            p = jnp.exp(sh - m)
            totals.append(jnp.sum(p, axis=-1, keepdims=True) + jnp.exp(sink - m))
            probs.append(p.astype(BF16))
        o = _dot(jnp.concatenate(probs, axis=0), v3)
        for i in range(A_SLABS):
            lo, hi = i, A_GROUP + i
            num = jnp.where(low_half, o[lo * A_BLK:(lo + 1) * A_BLK], o[hi * A_BLK:(hi + 1) * A_BLK])
            den = jnp.where(low_half, totals[lo], totals[hi])
            o_ref[0, pl.ds(row, A_BLK), i * LANES:(i + 1) * LANES] = (num / den).astype(BF16)
        return carry

    lax.fori_loop(0, A_TILE // A_BLK, block, 0)


def _attn_a(qa, ka, va, sink, bias):
    bsz, seq, _ = qa.shape
    qspec = pl.BlockSpec((1, A_TILE, QA_COLS), lambda b, i: (b, i, 0))
    kvspec = pl.BlockSpec((1, seq, LANES), lambda b, i: (b, 0, 0))
    return pl.pallas_call(
        functools.partial(_attn_a_kernel, seq=seq),
        grid=(bsz, seq // A_TILE),
        in_specs=[pl.BlockSpec(memory_space=pltpu.SMEM), qspec, kvspec, kvspec, _resident(bias.shape)],
        out_specs=qspec,
        out_shape=jax.ShapeDtypeStruct((bsz, seq, QA_COLS), BF16),
        compiler_params=_params(2),
        name="attn_a",
    )(sink, qa, ka, va, bias)


def _attn_a_bias():
    ar = jnp.arange
    return _band_bias(_alibi_slopes(A_HEADS), ar(A_BLK), ar(A_KEYS), (0, A_BLK, 2 * A_BLK), A_RADIUS, 1)


def _pair_attend(q2, k2, v2, bias, low_half):
    n = q2.shape[0]
    zero = jnp.zeros((), BF16)
    lhs = jnp.concatenate([jnp.where(low_half, q2, zero), jnp.where(low_half, zero, q2)], axis=0)
    s = _dot_nt(lhs, k2) + bias
    m = jnp.max(s, axis=-1, keepdims=True)
    p = jnp.exp(s - m)
    denom = jnp.sum(p, axis=-1, keepdims=True)
    o = _dot(p.astype(BF16), v2)
    denom = jnp.where(low_half, denom[:n], denom[n:])
    out = jnp.where(low_half, o[:n], o[n:]) / denom
    lse = jnp.where(low_half, m[:n], m[n:]) + jnp.log(denom)
    return out, lse


def _attn_b_kernel(q_ref, k_ref, v_ref, qr_ref, kr_ref, vr_ref, b1_ref, b4_ref, b16_ref, o_ref, osc, lsc,
                   *, seq):
    rows = seq // MAX_DIL
    qs, ks, vs = qr_ref.at[0, 0], kr_ref.at[0, 0], vr_ref.at[0, 0]
    low_half = lax.broadcasted_iota(jnp.int32, (1, LANES), 1) < HEAD_DIM

    def super_tile(st, carry):
        t0 = pl.multiple_of(st * SUPER, SUPER)
        u0 = pl.multiple_of(st * B_TILE, B_TILE)

        whole = rows == B_TILE
        keys16 = B_TILE if whole else B_KEYS
        win16 = 0 if whole else pl.multiple_of(jnp.clip(u0 - B_RADIUS, 0, rows - B_KEYS), B_RADIUS)

        def d16(r, c):
            q2 = qs[r, pl.ds(u0, B_TILE), :]
            k2 = ks[r, pl.ds(win16, keys16), :]
            v2 = vs[r, pl.ds(win16, keys16), :]
            bias = b16_ref[0, 0 if whole else _edge_variant(u0, rows - B_TILE)]
            out, lse = _pair_attend(q2, k2, v2, bias, low_half)
            osc[2, pl.ds(r, B_TILE, stride=MAX_DIL), :] = out
            lsc[2, pl.ds(r, B_TILE, stride=MAX_DIL), :] = lse
            return c

        lax.fori_loop(0, MAX_DIL, d16, 0, unroll=16)

        def d4(n, c):
            r4 = n // 4
            part = n % 4
            u1 = pl.multiple_of(u0 + part * 32, 32)
            win = pl.multiple_of(jnp.clip(u1 - 16, 0, rows - 64), 16)
            q2 = jnp.concatenate([qs[4 * a + r4, pl.ds(u1, 32), :] for a in range(4)], axis=0)
            k2 = jnp.concatenate([ks[4 * a + r4, pl.ds(win, 64), :] for a in range(4)], axis=0)
            v2 = jnp.concatenate([vs[4 * a + r4, pl.ds(win, 64), :] for a in range(4)], axis=0)
            out, lse = _pair_attend(q2, k2, v2, b4_ref[0, _edge_variant(u1, rows - 32)], low_half)
            for a in range(4):
                dst = pl.ds(part * (32 * MAX_DIL) + 4 * a + r4, 32, stride=MAX_DIL)
                osc[1, dst, :] = out[a * 32:(a + 1) * 32]
                lsc[1, dst, :] = lse[a * 32:(a + 1) * 32]
            return c

        lax.fori_loop(0, 16, d4, 0, unroll=16)

        def d1(n, c):
            off = pl.multiple_of(n * B_TILE, B_TILE)
            start = t0 + off
            win = pl.multiple_of(jnp.clip(start - B_RADIUS, 0, seq - B_KEYS), B_RADIUS)
            q2 = q_ref[0, 0, pl.ds(start, B_TILE), :]
            k2 = k_ref[0, 0, pl.ds(win, B_KEYS), :]
            v2 = v_ref[0, 0, pl.ds(win, B_KEYS), :]
            out, lse = _pair_attend(q2, k2, v2, b1_ref[0, _edge_variant(start, seq - B_TILE)], low_half)
            osc[0, pl.ds(off, B_TILE), :] = out
            lsc[0, pl.ds(off, B_TILE), :] = lse
            return c

        lax.fori_loop(0, SUPER // B_TILE, d1, 0, unroll=16)

        l0, l1, l2 = lsc[0], lsc[1], lsc[2]
        m = jnp.maximum(jnp.maximum(l0, l1), l2)
        e0, e1, e2 = jnp.exp(l0 - m), jnp.exp(l1 - m), jnp.exp(l2 - m)
        mixed = (e0 * osc[0] + e1 * osc[1] + e2 * osc[2]) / (e0 + e1 + e2)
        o_ref[0, 0, pl.ds(t0, SUPER), :] = mixed.astype(BF16)
        return carry

    lax.fori_loop(0, seq // SUPER, super_tile, 0)


def _attn_b(qkv, qkv_resid, biases):
    bsz, _, seq, _ = qkv[0].shape
    rows = seq // MAX_DIL
    spec = pl.BlockSpec((1, 1, seq, LANES), lambda hp, b: (b, hp, 0, 0))
    rspec = pl.BlockSpec((1, 1, MAX_DIL, rows, LANES), lambda hp, b: (b, hp, 0, 0, 0))
    bias_specs = [pl.BlockSpec((1,) + t.shape[1:], lambda hp, b: (hp, 0, 0, 0)) for t in biases]
    return pl.pallas_call(
        functools.partial(_attn_b_kernel, seq=seq),
        grid=(B_PAIRS, bsz),
        in_specs=[spec] * 3 + [rspec] * 3 + bias_specs,
        out_specs=spec,
        out_shape=jax.ShapeDtypeStruct((bsz, B_PAIRS, seq, LANES), BF16),
        scratch_shapes=[
            pltpu.VMEM((3, SUPER, LANES), F32),
            pltpu.VMEM((3, SUPER, LANES), F32),
        ],
        compiler_params=_params(2),
        name="attn_b",
    )(*qkv, *qkv_resid, *biases)


def _attn_b_biases(seq):
    slopes = _alibi_slopes(B_HEADS)
    ar = jnp.arange
    pairs = lambda make: jnp.stack([make(slopes[2 * hp:2 * hp + 2]) for hp in range(B_PAIRS)])
    b1 = pairs(lambda sl: _band_bias(sl, ar(B_TILE), ar(B_KEYS), (0, B_RADIUS, 2 * B_RADIUS), B_RADIUS, 1))
    a4 = ar(4)[:, None]
    q4 = (4 * ar(32)[None, :] + a4).reshape(-1)
    k4 = (4 * ar(64)[None, :] + a4).reshape(-1)
    b4 = pairs(lambda sl: _band_bias(sl, q4, k4, (0, 64, 128), B_RADIUS, 4))
    if seq // MAX_DIL == B_TILE:
        b16 = pairs(lambda sl: _band_bias(sl, ar(B_TILE), ar(B_TILE), (0,), B_RADIUS, 16))
    else:
        b16 = pairs(lambda sl: _band_bias(sl, ar(B_TILE), ar(B_KEYS), (0, B_RADIUS, 2 * B_RADIUS), B_RADIUS, 16))
    return b1, b4, b16


def _merge_ln_kernel(x_ref, oa_ref, ob_ref, km_ref, vm_ref, wqm_ref, wg_ref, wbr_ref, wo_ref,
                     g_ref, b_ref, o_ref):
    x = x_ref[0]
    xb = x.astype(BF16)
    qm = _dot(xb, wqm_ref[...]).astype(BF16)
    heads = []
    for h in range(M_HEADS):
        lanes = slice(h * M_HEAD_DIM, (h + 1) * M_HEAD_DIM)
        s = _dot_nt(qm[:, lanes], km_ref[0, :, lanes]) * (M_HEAD_DIM ** -0.5)
        p = jnp.exp(s - jnp.max(s, axis=-1, keepdims=True))
        denom = jnp.sum(p, axis=-1, keepdims=True)
        heads.append(_dot(p.astype(BF16), vm_ref[0, :, lanes]) / denom)
    om = jnp.concatenate(heads, axis=-1).astype(BF16)
    ob = jnp.concatenate([ob_ref[0, hp] for hp in range(B_PAIRS)], axis=-1)
    mixed = None
    for i, branch in enumerate((oa_ref[0], ob, om)):
        gate = jax.nn.sigmoid(_dot(xb, wg_ref[:, i * D_MODEL:(i + 1) * D_MODEL]))
        term = gate * _dot(branch, wbr_ref[i])
        mixed = term if mixed is None else mixed + term
    y = ALPHA * x + _dot(mixed.astype(BF16), wo_ref[...])
    o_ref[0] = _layer_norm(y, g_ref[...], b_ref[...])


def _merge_ln(x, oa, ob, km, vm, w_qm, w_gate, w_branch, w_out, g, b):
    bsz, seq, _ = x.shape
    rows = MERGE_ROWS
    tok = lambda cols: pl.BlockSpec((1, rows, cols), lambda bi, i: (bi, i, 0))
    mem = pl.BlockSpec((1, N_MEM, QM_COLS), lambda bi, i: (bi, 0, 0))
    return pl.pallas_call(
        _merge_ln_kernel,
        grid=(bsz, seq // rows),
        in_specs=[tok(D_MODEL), tok(QA_COLS),
                  pl.BlockSpec((1, B_PAIRS, rows, LANES), lambda bi, i: (bi, 0, i, 0)),
                  mem, mem,
                  _resident(w_qm.shape), _resident(w_gate.shape), _resident(w_branch.shape),
                  _resident(w_out.shape), _resident((1, D_MODEL)), _resident((1, D_MODEL))],
        out_specs=tok(D_MODEL),
        out_shape=jax.ShapeDtypeStruct((bsz, seq, D_MODEL), F32),
        compiler_params=_params(2),
        name="merge_ln",
    )(x, oa, ob, km, vm, w_qm, w_gate, w_branch, w_out, g, b)


def _slab_order():
    cols = []
    for i in range(A_GROUP):
        cols += list(range(i * HEAD_DIM, (i + 1) * HEAD_DIM))
        cols += list(range((A_GROUP + i) * HEAD_DIM, (A_GROUP + i + 1) * HEAD_DIM))
    return jnp.asarray(cols, dtype=jnp.int32)


def kernel(x_prompt, x_sample, mem_prompt, mem_sample, ffn1_w_in, ffn1_w_out, ln1_g, ln1_b, w_in, w_mem_kv,
           sink_a, w_branch, w_out, ln2_g, ln2_b, ffn2_w_in, ffn2_w_out, ln3_g, ln3_b):
    order = _slab_order()
    bias_a = _attn_a_bias()
    q_scale = HEAD_DIM ** -0.5

    layers = []
    for l in range(DEPTH):
        w = w_in[l]
        qa_w = w[:, :QA_COLS][:, order] * q_scale
        kva_w = w[:, QA_COLS:QA_COLS + KVA_COLS]
        qb_w = w[:, QA_COLS + KVA_COLS:QA_COLS + KVA_COLS + BRANCH_WIDTH] * q_scale
        kvb_w = w[:, QA_COLS + KVA_COLS + BRANCH_WIDTH:QM_START]
        w_proj = jnp.concatenate([qa_w, kva_w, qb_w, kvb_w], axis=1).astype(BF16)
        w_br = jnp.concatenate([w_branch[l, :1][:, order], w_branch[l, 1:]], axis=0).astype(BF16)
        row = lambda t: t[l].reshape(1, D_MODEL).astype(F32)
        layers.append(dict(
            ffn1=(ffn1_w_in[l].astype(BF16), ffn1_w_out[l].astype(BF16), row(ln1_g), row(ln1_b)),
            ffn2=(ffn2_w_in[l].astype(BF16), ffn2_w_out[l].astype(BF16), row(ln3_g), row(ln3_b)),
            w_proj=w_proj,
            w_mem=w_mem_kv[l].astype(BF16),
            sink=sink_a[l].astype(F32),
            merge=(w[:, QM_START:GATE_START].astype(BF16), w[:, GATE_START:].astype(BF16), w_br,
                   w_out[l].astype(BF16), row(ln2_g), row(ln2_b)),
        ))

    def trunk(x, mem):
        bsz, seq, _ = x.shape
        biases_b = _attn_b_biases(seq)
        for p in layers:
            x = _ffn_ln(x.reshape(bsz * seq, D_MODEL), *p["ffn1"]).reshape(bsz, seq, D_MODEL)
            qa, ka, va, *qkv_b = _in_proj(x, p["w_proj"])
            km, vm = _mem_kv(mem, p["w_mem"])
            oa = _attn_a(qa, ka, va, p["sink"], bias_a)
            ob = _attn_b(qkv_b[:3], qkv_b[3:], biases_b)
            x = _merge_ln(x, oa, ob, km, vm, *p["merge"])
            x = _ffn_ln(x.reshape(bsz * seq, D_MODEL), *p["ffn2"]).reshape(bsz, seq, D_MODEL)
        return x

    return trunk(x_prompt, mem_prompt), trunk(x_sample, mem_sample)
```

```python
import functools

import jax
import jax.numpy as jnp
from jax import lax
from jax.experimental import pallas as pl
from jax.experimental.pallas import tpu as pltpu

F32 = jnp.float32
BF16 = jnp.bfloat16

D_MODEL = 1024
DEPTH = 2
HEAD_DIM = 64
A_HEADS = 8
A_KV_HEADS = 2
A_GROUP = A_HEADS // A_KV_HEADS
A_RADIUS = 128
B_HEADS = 8
B_RADIUS = 64
M_HEADS = 4
M_HEAD_DIM = 128
N_MEM = 256
N_BRANCH = 3
BRANCH_WIDTH = 512
D_FF = 2816
LN_EPS = 1e-5
ALPHA = (2 * DEPTH) ** 0.25
NEG_INF = -1e30

QA_COLS = A_HEADS * HEAD_DIM
KVA_COLS = 2 * A_KV_HEADS * HEAD_DIM
QKVB_COLS = 3 * B_HEADS * HEAD_DIM
QM_COLS = M_HEADS * M_HEAD_DIM
GATE_COLS = N_BRANCH * D_MODEL
QM_START = QA_COLS + KVA_COLS + QKVB_COLS
GATE_START = QM_START + QM_COLS

LANES = 128
MAX_DIL = 16
B_PAIRS = B_HEADS * HEAD_DIM // LANES
A_SLABS = QA_COLS // LANES
PROJ_COLS = QA_COLS + KVA_COLS + QKVB_COLS

FFN_ROWS = 1024
FFN_CHUNK = 256
PROJ_ROWS = 512
STAGE_PITCH = 40
MERGE_ROWS = 512
A_TILE = 1024
A_BLK = A_RADIUS
A_KEYS = 3 * A_BLK
B_TILE = 128
B_KEYS = 2 * B_TILE
SUPER = B_TILE * MAX_DIL
GROUP_PITCH = 24

VMEM_LIMIT = 56 * 1024 * 1024


def _resident(shape):
    zeros = (0,) * len(shape)
    return pl.BlockSpec(shape, lambda *_: zeros, pipeline_mode=pl.Buffered(1))


def _params(n_axes):
    return pltpu.CompilerParams(dimension_semantics=("parallel",) * n_axes,
                                vmem_limit_bytes=VMEM_LIMIT)


def _layer_norm(y, g, b):
    mu = jnp.mean(y, axis=-1, keepdims=True)
    d = y - mu
    var = jnp.mean(d * d, axis=-1, keepdims=True)
    return d * lax.rsqrt(var + LN_EPS) * g + b


def _dot(a, b):
    return jnp.dot(a, b, preferred_element_type=F32)


def _dot_nt(a, b):
    return lax.dot_general(a, b, (((1,), (1,)), ((), ())), preferred_element_type=F32)


def _alibi_slopes(n):
    return 2.0 ** (-8.0 * jnp.arange(1, n + 1, dtype=F32) / n)


def _ffn_ln_kernel(x_ref, win_ref, wout_ref, g_ref, b_ref, o_ref):
    x = x_ref[...]
    xb = x.astype(BF16)
    acc = None
    for c in range(D_FF // FFN_CHUNK):
        lo = c * FFN_CHUNK
        gate = _dot(xb, win_ref[:, lo:lo + FFN_CHUNK])
        up = _dot(xb, win_ref[:, D_FF + lo:D_FF + lo + FFN_CHUNK])
        act = (gate * jax.nn.sigmoid(gate) * up).astype(BF16)
        part = _dot(act, wout_ref[lo:lo + FFN_CHUNK, :])
        acc = part if acc is None else acc + part
    o_ref[...] = _layer_norm(ALPHA * x + 0.5 * acc, g_ref[...], b_ref[...])


def _ffn_ln(x2d, w_in, w_out, g, b):
    rows = x2d.shape[0]
    return pl.pallas_call(
        _ffn_ln_kernel,
        grid=(rows // FFN_ROWS,),
        in_specs=[
            pl.BlockSpec((FFN_ROWS, D_MODEL), lambda i: (i, 0)),
            _resident((D_MODEL, 2 * D_FF)),
            _resident((D_FF, D_MODEL)),
            _resident((1, D_MODEL)),
            _resident((1, D_MODEL)),
        ],
        out_specs=pl.BlockSpec((FFN_ROWS, D_MODEL), lambda i: (i, 0)),
        out_shape=jax.ShapeDtypeStruct((rows, D_MODEL), F32),
        compiler_params=_params(1),
        name="ffn_ln",
    )(x2d, w_in, w_out, g, b)


def _in_proj_kernel(x_ref, w_ref, qa_ref, ka_ref, va_ref, qb_ref, kb_ref, vb_ref,
                    qr_ref, kr_ref, vr_ref, stage):
    xb = x_ref[0].astype(BF16)
    base = QA_COLS + KVA_COLS
    sub = PROJ_ROWS // MAX_DIL
    for n, (ref, res_ref) in enumerate(((qb_ref, qr_ref), (kb_ref, kr_ref), (vb_ref, vr_ref))):
        lo = base + n * BRANCH_WIDTH
        res = _dot(xb, w_ref[:, lo:lo + BRANCH_WIDTH])
        for hp in range(B_PAIRS):
            slab = res[:, hp * LANES:(hp + 1) * LANES]
            ref[0, hp] = slab.astype(BF16)
            for u in range(sub):
                stage[n, hp, pl.ds(u, MAX_DIL, stride=STAGE_PITCH), :] = slab[u * MAX_DIL:(u + 1) * MAX_DIL]
            for r in range(MAX_DIL):
                res_ref[0, hp, r] = stage[n, hp, r * STAGE_PITCH:r * STAGE_PITCH + sub, :].astype(BF16)
    kva = _dot(xb, w_ref[:, QA_COLS:base]).astype(BF16)
    ka_ref[0] = kva[:, :LANES]
    va_ref[0] = kva[:, LANES:]
    qa_ref[0] = _dot(xb, w_ref[:, 0:QA_COLS]).astype(BF16)


def _in_proj(x, w):
    bsz, seq, _ = x.shape
    rows = PROJ_ROWS
    sub = rows // MAX_DIL
    tok = lambda cols: pl.BlockSpec((1, rows, cols), lambda b, i: (b, i, 0))
    pair = pl.BlockSpec((1, B_PAIRS, rows, LANES), lambda b, i: (b, 0, i, 0))
    pair_shape = jax.ShapeDtypeStruct((bsz, B_PAIRS, seq, LANES), BF16)
    resid = pl.BlockSpec((1, B_PAIRS, MAX_DIL, sub, LANES), lambda b, i: (b, 0, 0, i, 0))
    resid_shape = jax.ShapeDtypeStruct((bsz, B_PAIRS, MAX_DIL, seq // MAX_DIL, LANES), BF16)
    return pl.pallas_call(
        _in_proj_kernel,
        grid=(bsz, seq // rows),
        in_specs=[tok(D_MODEL), _resident((D_MODEL, PROJ_COLS))],
        out_specs=[tok(QA_COLS), tok(LANES), tok(LANES), pair, pair, pair, resid, resid, resid],
        out_shape=[
            jax.ShapeDtypeStruct((bsz, seq, QA_COLS), BF16),
            jax.ShapeDtypeStruct((bsz, seq, LANES), BF16),
            jax.ShapeDtypeStruct((bsz, seq, LANES), BF16),
            pair_shape, pair_shape, pair_shape, resid_shape, resid_shape, resid_shape,
        ],
        scratch_shapes=[pltpu.VMEM((3, B_PAIRS, MAX_DIL * STAGE_PITCH, LANES), F32)],
        compiler_params=_params(2),
        name="in_proj",
    )(x, w)


def _mem_kv_kernel(mem_ref, w_ref, km_ref, vm_ref):
    res = _dot(mem_ref[0].astype(BF16), w_ref[...]).astype(BF16)
    km_ref[0] = res[:, :QM_COLS]
    vm_ref[0] = res[:, QM_COLS:]


def _mem_kv(mem, w):
    bsz = mem.shape[0]
    out = pl.BlockSpec((1, N_MEM, QM_COLS), lambda b: (b, 0, 0))
    shape = jax.ShapeDtypeStruct((bsz, N_MEM, QM_COLS), BF16)
    return pl.pallas_call(
        _mem_kv_kernel,
        grid=(bsz,),
        in_specs=[pl.BlockSpec((1, N_MEM, D_MODEL), lambda b: (b, 0, 0)),
                  _resident((D_MODEL, 2 * QM_COLS))],
        out_specs=[out, out],
        out_shape=[shape, shape],
        compiler_params=_params(1),
        name="mem_kv",
    )(mem, w)


def _edge_variant(start, last_start):
    return jnp.where(start == 0, 0, jnp.where(start == last_start, 2, 1))


def _band_bias(slopes, q_pos, k_pos, shifts, radius, dil):
    tables = []
    for shift in shifts:
        rel = k_pos[None, :] - shift - q_pos[:, None]
        dist = jnp.abs(rel).astype(F32) * dil
        table = jnp.where((jnp.abs(rel) <= radius)[None], -slopes[:, None, None] * dist[None], NEG_INF)
        tables.append(table.reshape(-1, k_pos.shape[0]))
    return jnp.stack(tables)


def _attn_a_kernel(sink_ref, q_ref, k_ref, v_ref, bias_ref, o_ref, *, seq):
    tile = pl.program_id(1)
    low_half = lax.broadcasted_iota(jnp.int32, (1, LANES), 1) < HEAD_DIM
    zero = jnp.zeros((), BF16)

    def block(j, carry):
        row = pl.multiple_of(j * A_BLK, A_BLK)
        start = tile * A_TILE + row
        window = pl.multiple_of(jnp.clip(start - A_BLK, 0, seq - A_KEYS), A_BLK)
        q4 = q_ref[0, pl.ds(row, A_BLK), :]
        k3 = k_ref[0, pl.ds(window, A_KEYS), :]
        v3 = v_ref[0, pl.ds(window, A_KEYS), :]
        lhs = jnp.concatenate(
            [jnp.where(low_half if g == 0 else jnp.logical_not(low_half),
                       q4[:, i * LANES:(i + 1) * LANES], zero)
             for g in range(A_KV_HEADS) for i in range(A_SLABS)], axis=0)
        s = _dot_nt(lhs, k3) + bias_ref[_edge_variant(start, seq - A_BLK)]
        probs, totals = [], []
        for h in range(A_HEADS):
            sh = s[h * A_BLK:(h + 1) * A_BLK]
            sink = sink_ref[h]
            m = jnp.maximum(jnp.max(sh, axis=-1, keepdims=True), sink)
            p = jnp.exp(sh - m)
            totals.append(jnp.sum(p, axis=-1, keepdims=True) + jnp.exp(sink - m))
            probs.append(p.astype(BF16))
        o = _dot(jnp.concatenate(probs, axis=0), v3)
        for i in range(A_SLABS):
            lo, hi = i, A_GROUP + i
            num = jnp.where(low_half, o[lo * A_BLK:(lo + 1) * A_BLK], o[hi * A_BLK:(hi + 1) * A_BLK])
            den = jnp.where(low_half, totals[lo], totals[hi])
            o_ref[0, pl.ds(row, A_BLK), i * LANES:(i + 1) * LANES] = (num / den).astype(BF16)
        return carry

    lax.fori_loop(0, A_TILE // A_BLK, block, 0, unroll=2)


def _attn_a(qa, ka, va, sink, bias):
    bsz, seq, _ = qa.shape
    qspec = pl.BlockSpec((1, A_TILE, QA_COLS), lambda b, i: (b, i, 0))
    kvspec = pl.BlockSpec((1, seq, LANES), lambda b, i: (b, 0, 0))
    return pl.pallas_call(
        functools.partial(_attn_a_kernel, seq=seq),
        grid=(bsz, seq // A_TILE),
        in_specs=[pl.BlockSpec(memory_space=pltpu.SMEM), qspec, kvspec, kvspec, _resident(bias.shape)],
        out_specs=qspec,
        out_shape=jax.ShapeDtypeStruct((bsz, seq, QA_COLS), BF16),
        compiler_params=_params(2),
        name="attn_a",
    )(sink, qa, ka, va, bias)


def _attn_a_bias():
    ar = jnp.arange
    return _band_bias(_alibi_slopes(A_HEADS), ar(A_BLK), ar(A_KEYS), (0, A_BLK, 2 * A_BLK), A_RADIUS, 1)


def _pair_attend(q2, k2, v2, bias, low_half):
    n = q2.shape[0]
    zero = jnp.zeros((), BF16)
    lhs = jnp.concatenate([jnp.where(low_half, q2, zero), jnp.where(low_half, zero, q2)], axis=0)
    s = _dot_nt(lhs, k2) + bias
    m = jnp.max(s, axis=-1, keepdims=True)
    p = jnp.exp(s - m)
    denom = jnp.sum(p, axis=-1, keepdims=True)
    o = _dot(p.astype(BF16), v2)
    denom = jnp.where(low_half, denom[:n], denom[n:])
    out = jnp.where(low_half, o[:n], o[n:]) / denom
    lse = jnp.where(low_half, m[:n], m[n:]) + jnp.log(denom)
    return out, lse


def _attn_b_kernel(q_ref, k_ref, v_ref, qr_ref, kr_ref, vr_ref, b1_ref, b4_ref, b16_ref, o_ref, osc, lsc,
                   *, seq):
    rows = seq // MAX_DIL
    qs, ks, vs = qr_ref.at[0, 0], kr_ref.at[0, 0], vr_ref.at[0, 0]
    low_half = lax.broadcasted_iota(jnp.int32, (1, LANES), 1) < HEAD_DIM
    groups_per_tile = B_TILE // MAX_DIL

    def super_tile(st, carry):
        t0 = pl.multiple_of(st * SUPER, SUPER)
        u0 = pl.multiple_of(st * B_TILE, B_TILE)

        whole = rows == B_TILE
        keys16 = B_TILE if whole else B_KEYS
        win16 = 0 if whole else pl.multiple_of(jnp.clip(u0 - B_RADIUS, 0, rows - B_KEYS), B_RADIUS)

        def d16(r, c):
            q2 = qs[r, pl.ds(u0, B_TILE), :]
            k2 = ks[r, pl.ds(win16, keys16), :]
            v2 = vs[r, pl.ds(win16, keys16), :]
            bias = b16_ref[0, 0 if whole else _edge_variant(u0, rows - B_TILE)]
            out, lse = _pair_attend(q2, k2, v2, bias, low_half)
            osc[2, pl.ds(r, B_TILE, stride=GROUP_PITCH), :] = out
            lsc[2, pl.ds(r, B_TILE, stride=GROUP_PITCH), :] = lse
            return c

        lax.fori_loop(0, MAX_DIL, d16, 0, unroll=16)

        def d4(n, c):
            r4 = n // 4
            part = n % 4
            u1 = pl.multiple_of(u0 + part * 32, 32)
            win = pl.multiple_of(jnp.clip(u1 - 16, 0, rows - 64), 16)
            q2 = jnp.concatenate([qs[4 * a + r4, pl.ds(u1, 32), :] for a in range(4)], axis=0)
            k2 = jnp.concatenate([ks[4 * a + r4, pl.ds(win, 64), :] for a in range(4)], axis=0)
            v2 = jnp.concatenate([vs[4 * a + r4, pl.ds(win, 64), :] for a in range(4)], axis=0)
            out, lse = _pair_attend(q2, k2, v2, b4_ref[0, _edge_variant(u1, rows - 32)], low_half)
            for a in range(4):
                dst = pl.ds(part * (32 * GROUP_PITCH) + 4 * a + r4, 32, stride=GROUP_PITCH)
                osc[1, dst, :] = out[a * 32:(a + 1) * 32]
                lsc[1, dst, :] = lse[a * 32:(a + 1) * 32]
            return c

        lax.fori_loop(0, 16, d4, 0, unroll=16)

        def d1(n, c):
            off = pl.multiple_of(n * B_TILE, B_TILE)
            start = t0 + off
            win = pl.multiple_of(jnp.clip(start - B_RADIUS, 0, seq - B_KEYS), B_RADIUS)
            q2 = q_ref[0, 0, pl.ds(start, B_TILE), :]
            k2 = k_ref[0, 0, pl.ds(win, B_KEYS), :]
            v2 = v_ref[0, 0, pl.ds(win, B_KEYS), :]
            out, lse = _pair_attend(q2, k2, v2, b1_ref[0, _edge_variant(start, seq - B_TILE)], low_half)
            base = pl.multiple_of(n * (groups_per_tile * GROUP_PITCH), 8)
            for g in range(groups_per_tile):
                dst = pl.ds(base + g * GROUP_PITCH, MAX_DIL)
                osc[0, dst, :] = out[g * MAX_DIL:(g + 1) * MAX_DIL]
                lsc[0, dst, :] = lse[g * MAX_DIL:(g + 1) * MAX_DIL]
            return c

        lax.fori_loop(0, SUPER // B_TILE, d1, 0, unroll=16)

        def groups(ref, c):
            return jnp.concatenate(
                [ref[c, g * GROUP_PITCH:g * GROUP_PITCH + MAX_DIL, :] for g in range(SUPER // MAX_DIL)], axis=0)

        l0, l1, l2 = groups(lsc, 0), groups(lsc, 1), groups(lsc, 2)
        m = jnp.maximum(jnp.maximum(l0, l1), l2)
        e0, e1, e2 = jnp.exp(l0 - m), jnp.exp(l1 - m), jnp.exp(l2 - m)
        mixed = (e0 * groups(osc, 0) + e1 * groups(osc, 1) + e2 * groups(osc, 2)) / (e0 + e1 + e2)
        o_ref[0, 0, pl.ds(t0, SUPER), :] = mixed.astype(BF16)
        return carry

    lax.fori_loop(0, seq // SUPER, super_tile, 0)


def _attn_b(qkv, qkv_resid, biases):
    bsz, _, seq, _ = qkv[0].shape
    rows = seq // MAX_DIL
    spec = pl.BlockSpec((1, 1, seq, LANES), lambda hp, b: (b, hp, 0, 0))
    rspec = pl.BlockSpec((1, 1, MAX_DIL, rows, LANES), lambda hp, b: (b, hp, 0, 0, 0))
    bias_specs = [pl.BlockSpec((1,) + t.shape[1:], lambda hp, b: (hp, 0, 0, 0)) for t in biases]
    return pl.pallas_call(
        functools.partial(_attn_b_kernel, seq=seq),
        grid=(B_PAIRS, bsz),
        in_specs=[spec] * 3 + [rspec] * 3 + bias_specs,
        out_specs=spec,
        out_shape=jax.ShapeDtypeStruct((bsz, B_PAIRS, seq, LANES), BF16),
        scratch_shapes=[
            pltpu.VMEM((3, SUPER // MAX_DIL * GROUP_PITCH, LANES), F32),
            pltpu.VMEM((3, SUPER // MAX_DIL * GROUP_PITCH, LANES), F32),
        ],
        compiler_params=_params(2),
        name="attn_b",
    )(*qkv, *qkv_resid, *biases)


def _attn_b_biases(seq):
    slopes = _alibi_slopes(B_HEADS)
    ar = jnp.arange
    pairs = lambda make: jnp.stack([make(slopes[2 * hp:2 * hp + 2]) for hp in range(B_PAIRS)])
    b1 = pairs(lambda sl: _band_bias(sl, ar(B_TILE), ar(B_KEYS), (0, B_RADIUS, 2 * B_RADIUS), B_RADIUS, 1))
    a4 = ar(4)[:, None]
    q4 = (4 * ar(32)[None, :] + a4).reshape(-1)
    k4 = (4 * ar(64)[None, :] + a4).reshape(-1)
    b4 = pairs(lambda sl: _band_bias(sl, q4, k4, (0, 64, 128), B_RADIUS, 4))
    if seq // MAX_DIL == B_TILE:
        b16 = pairs(lambda sl: _band_bias(sl, ar(B_TILE), ar(B_TILE), (0,), B_RADIUS, 16))
    else:
        b16 = pairs(lambda sl: _band_bias(sl, ar(B_TILE), ar(B_KEYS), (0, B_RADIUS, 2 * B_RADIUS), B_RADIUS, 16))
    return b1, b4, b16


def _merge_ln_kernel(x_ref, oa_ref, ob_ref, km_ref, vm_ref, wqm_ref, wg_ref, wbr_ref, wo_ref,
                     g_ref, b_ref, o_ref):
    x = x_ref[0]
    xb = x.astype(BF16)
    qm = _dot(xb, wqm_ref[...]).astype(BF16)
    heads = []
    for h in range(M_HEADS):
        lanes = slice(h * M_HEAD_DIM, (h + 1) * M_HEAD_DIM)
        s = _dot_nt(qm[:, lanes], km_ref[0, :, lanes]) * (M_HEAD_DIM ** -0.5)
        p = jnp.exp(s - jnp.max(s, axis=-1, keepdims=True))
        denom = jnp.sum(p, axis=-1, keepdims=True)
        heads.append(_dot(p.astype(BF16), vm_ref[0, :, lanes]) / denom)
    om = jnp.concatenate(heads, axis=-1).astype(BF16)
    ob = jnp.concatenate([ob_ref[0, hp] for hp in range(B_PAIRS)], axis=-1)
    mixed = None
    for i, branch in enumerate((oa_ref[0], ob, om)):
        gate = jax.nn.sigmoid(_dot(xb, wg_ref[:, i * D_MODEL:(i + 1) * D_MODEL]))
        term = gate * _dot(branch, wbr_ref[i])
        mixed = term if mixed is None else mixed + term
    y = ALPHA * x + _dot(mixed.astype(BF16), wo_ref[...])
    o_ref[0] = _layer_norm(y, g_ref[...], b_ref[...])


def _merge_ln(x, oa, ob, km, vm, w_qm, w_gate, w_branch, w_out, g, b):
    bsz, seq, _ = x.shape
    rows = MERGE_ROWS
    tok = lambda cols: pl.BlockSpec((1, rows, cols), lambda bi, i: (bi, i, 0))
    mem = pl.BlockSpec((1, N_MEM, QM_COLS), lambda bi, i: (bi, 0, 0))
    return pl.pallas_call(
        _merge_ln_kernel,
        grid=(bsz, seq // rows),
        in_specs=[tok(D_MODEL), tok(QA_COLS),
                  pl.BlockSpec((1, B_PAIRS, rows, LANES), lambda bi, i: (bi, 0, i, 0)),
                  mem, mem,
                  _resident(w_qm.shape), _resident(w_gate.shape), _resident(w_branch.shape),
                  _resident(w_out.shape), _resident((1, D_MODEL)), _resident((1, D_MODEL))],
        out_specs=tok(D_MODEL),
        out_shape=jax.ShapeDtypeStruct((bsz, seq, D_MODEL), F32),
        compiler_params=_params(2),
        name="merge_ln",
    )(x, oa, ob, km, vm, w_qm, w_gate, w_branch, w_out, g, b)


def _slab_order():
    cols = []
    for i in range(A_GROUP):
        cols += list(range(i * HEAD_DIM, (i + 1) * HEAD_DIM))
        cols += list(range((A_GROUP + i) * HEAD_DIM, (A_GROUP + i + 1) * HEAD_DIM))
    return jnp.asarray(cols, dtype=jnp.int32)


def kernel(x_prompt, x_sample, mem_prompt, mem_sample, ffn1_w_in, ffn1_w_out, ln1_g, ln1_b, w_in, w_mem_kv,
           sink_a, w_branch, w_out, ln2_g, ln2_b, ffn2_w_in, ffn2_w_out, ln3_g, ln3_b):
    order = _slab_order()
    bias_a = _attn_a_bias()
    q_scale = HEAD_DIM ** -0.5

    layers = []
    for l in range(DEPTH):
        w = w_in[l]
        qa_w = w[:, :QA_COLS][:, order] * q_scale
        kva_w = w[:, QA_COLS:QA_COLS + KVA_COLS]
        qb_w = w[:, QA_COLS + KVA_COLS:QA_COLS + KVA_COLS + BRANCH_WIDTH] * q_scale
        kvb_w = w[:, QA_COLS + KVA_COLS + BRANCH_WIDTH:QM_START]
        w_proj = jnp.concatenate([qa_w, kva_w, qb_w, kvb_w], axis=1).astype(BF16)
        w_br = jnp.concatenate([w_branch[l, :1][:, order], w_branch[l, 1:]], axis=0).astype(BF16)
        row = lambda t: t[l].reshape(1, D_MODEL).astype(F32)
        layers.append(dict(
            ffn1=(ffn1_w_in[l].astype(BF16), ffn1_w_out[l].astype(BF16), row(ln1_g), row(ln1_b)),
            ffn2=(ffn2_w_in[l].astype(BF16), ffn2_w_out[l].astype(BF16), row(ln3_g), row(ln3_b)),
            w_proj=w_proj,
            w_mem=w_mem_kv[l].astype(BF16),
            sink=sink_a[l].astype(F32),
            merge=(w[:, QM_START:GATE_START].astype(BF16), w[:, GATE_START:].astype(BF16), w_br,
                   w_out[l].astype(BF16), row(ln2_g), row(ln2_b)),
        ))

    def trunk(x, mem):
        bsz, seq, _ = x.shape
        biases_b = _attn_b_biases(seq)
        for p in layers:
            x = _ffn_ln(x.reshape(bsz * seq, D_MODEL), *p["ffn1"]).reshape(bsz, seq, D_MODEL)
            qa, ka, va, *qkv_b = _in_proj(x, p["w_proj"])
            km, vm = _mem_kv(mem, p["w_mem"])
            oa = _attn_a(qa, ka, va, p["sink"], bias_a)
            ob = _attn_b(qkv_b[:3], qkv_b[3:], biases_b)
            x = _merge_ln(x, oa, ob, km, vm, *p["merge"])
            x = _ffn_ln(x.reshape(bsz * seq, D_MODEL), *p["ffn2"]).reshape(bsz, seq, D_MODEL)
        return x

    return trunk(x_prompt, mem_prompt), trunk(x_sample, mem_sample)
```

```python
import functools

import jax
import jax.numpy as jnp
from jax import lax
from jax.experimental import pallas as pl
from jax.experimental.pallas import tpu as pltpu

F32 = jnp.float32
BF16 = jnp.bfloat16

D_MODEL = 1024
DEPTH = 2
HEAD_DIM = 64
A_HEADS = 8
A_KV_HEADS = 2
A_GROUP = A_HEADS // A_KV_HEADS
A_RADIUS = 128
B_HEADS = 8
B_RADIUS = 64
M_HEADS = 4
M_HEAD_DIM = 128
N_MEM = 256
N_BRANCH = 3
BRANCH_WIDTH = 512
D_FF = 2816
LN_EPS = 1e-5
ALPHA = (2 * DEPTH) ** 0.25
NEG_INF = -1e30
LOG2E = 1.4426950408889634

QA_COLS = A_HEADS * HEAD_DIM
KVA_COLS = 2 * A_KV_HEADS * HEAD_DIM
QKVB_COLS = 3 * B_HEADS * HEAD_DIM
QM_COLS = M_HEADS * M_HEAD_DIM
GATE_COLS = N_BRANCH * D_MODEL
QM_START = QA_COLS + KVA_COLS + QKVB_COLS
GATE_START = QM_START + QM_COLS

LANES = 128
MAX_DIL = 16
B_PAIRS = B_HEADS * HEAD_DIM // LANES
A_SLABS = QA_COLS // LANES
PROJ_COLS = QA_COLS + KVA_COLS + QKVB_COLS

FFN_ROWS = 1024
FFN_CHUNK = 256
PROJ_ROWS = 512
STAGE_PITCH = 40
MERGE_ROWS = 512
A_TILE = 1024
A_BLK = A_RADIUS
A_KEYS = 3 * A_BLK
B_TILE = 128
B_KEYS = 2 * B_TILE
SUPER = B_TILE * MAX_DIL
GROUP_PITCH = 24

VMEM_LIMIT = 56 * 1024 * 1024


def _resident(shape):
    zeros = (0,) * len(shape)
    return pl.BlockSpec(shape, lambda *_: zeros, pipeline_mode=pl.Buffered(1))


def _params(n_axes):
    return pltpu.CompilerParams(dimension_semantics=("parallel",) * n_axes,
                                vmem_limit_bytes=VMEM_LIMIT)


def _layer_norm(y, g, b):
    mu = jnp.mean(y, axis=-1, keepdims=True)
    d = y - mu
    var = jnp.mean(d * d, axis=-1, keepdims=True)
    return d * lax.rsqrt(var + LN_EPS) * g + b


def _dot(a, b):
    return jnp.dot(a, b, preferred_element_type=F32)


def _dot_nt(a, b):
    return lax.dot_general(a, b, (((1,), (1,)), ((), ())), preferred_element_type=F32)


def _alibi_slopes(n):
    return 2.0 ** (-8.0 * jnp.arange(1, n + 1, dtype=F32) / n)


def _ffn_ln_kernel(x_ref, win_ref, wout_ref, g_ref, b_ref, o_ref):
    x = x_ref[...]
    xb = x.astype(BF16)
    acc = None
    for c in range(D_FF // FFN_CHUNK):
        lo = c * FFN_CHUNK
        gate = _dot(xb, win_ref[:, lo:lo + FFN_CHUNK])
        up = _dot(xb, win_ref[:, D_FF + lo:D_FF + lo + FFN_CHUNK])
        act = (gate * jax.nn.sigmoid(gate) * up).astype(BF16)
        part = _dot(act, wout_ref[lo:lo + FFN_CHUNK, :])
        acc = part if acc is None else acc + part
    o_ref[...] = _layer_norm(ALPHA * x + 0.5 * acc, g_ref[...], b_ref[...])


def _ffn_ln(x2d, w_in, w_out, g, b):
    rows = x2d.shape[0]
    return pl.pallas_call(
        _ffn_ln_kernel,
        grid=(rows // FFN_ROWS,),
        in_specs=[
            pl.BlockSpec((FFN_ROWS, D_MODEL), lambda i: (i, 0)),
            _resident((D_MODEL, 2 * D_FF)),
            _resident((D_FF, D_MODEL)),
            _resident((1, D_MODEL)),
            _resident((1, D_MODEL)),
        ],
        out_specs=pl.BlockSpec((FFN_ROWS, D_MODEL), lambda i: (i, 0)),
        out_shape=jax.ShapeDtypeStruct((rows, D_MODEL), F32),
        compiler_params=_params(1),
        name="ffn_ln",
    )(x2d, w_in, w_out, g, b)


def _in_proj_kernel(x_ref, w_ref, qa_ref, ka_ref, va_ref, qb_ref, kb_ref, vb_ref,
                    qr_ref, kr_ref, vr_ref, stage):
    xb = x_ref[0].astype(BF16)
    base = QA_COLS + KVA_COLS
    sub = PROJ_ROWS // MAX_DIL
    for n, (ref, res_ref) in enumerate(((qb_ref, qr_ref), (kb_ref, kr_ref), (vb_ref, vr_ref))):
        lo = base + n * BRANCH_WIDTH
        res = _dot(xb, w_ref[:, lo:lo + BRANCH_WIDTH])
        for hp in range(B_PAIRS):
            slab = res[:, hp * LANES:(hp + 1) * LANES]
            ref[0, hp] = slab.astype(BF16)
            for u in range(sub):
                stage[n, hp, pl.ds(u, MAX_DIL, stride=STAGE_PITCH), :] = slab[u * MAX_DIL:(u + 1) * MAX_DIL]
            for r in range(MAX_DIL):
                res_ref[0, hp, r] = stage[n, hp, r * STAGE_PITCH:r * STAGE_PITCH + sub, :].astype(BF16)
    kva = _dot(xb, w_ref[:, QA_COLS:base]).astype(BF16)
    ka_ref[0] = kva[:, :LANES]
    va_ref[0] = kva[:, LANES:]
    qa_ref[0] = _dot(xb, w_ref[:, 0:QA_COLS]).astype(BF16)


def _in_proj(x, w):
    bsz, seq, _ = x.shape
    rows = PROJ_ROWS
    sub = rows // MAX_DIL
    tok = lambda cols: pl.BlockSpec((1, rows, cols), lambda b, i: (b, i, 0))
    pair = pl.BlockSpec((1, B_PAIRS, rows, LANES), lambda b, i: (b, 0, i, 0))
    pair_shape = jax.ShapeDtypeStruct((bsz, B_PAIRS, seq, LANES), BF16)
    resid = pl.BlockSpec((1, B_PAIRS, MAX_DIL, sub, LANES), lambda b, i: (b, 0, 0, i, 0))
    resid_shape = jax.ShapeDtypeStruct((bsz, B_PAIRS, MAX_DIL, seq // MAX_DIL, LANES), BF16)
    return pl.pallas_call(
        _in_proj_kernel,
        grid=(bsz, seq // rows),
        in_specs=[tok(D_MODEL), _resident((D_MODEL, PROJ_COLS))],
        out_specs=[tok(QA_COLS), tok(LANES), tok(LANES), pair, pair, pair, resid, resid, resid],
        out_shape=[
            jax.ShapeDtypeStruct((bsz, seq, QA_COLS), BF16),
            jax.ShapeDtypeStruct((bsz, seq, LANES), BF16),
            jax.ShapeDtypeStruct((bsz, seq, LANES), BF16),
            pair_shape, pair_shape, pair_shape, resid_shape, resid_shape, resid_shape,
        ],
        scratch_shapes=[pltpu.VMEM((3, B_PAIRS, MAX_DIL * STAGE_PITCH, LANES), F32)],
        compiler_params=_params(2),
        name="in_proj",
    )(x, w)


def _mem_kv_kernel(mem_ref, w_ref, km_ref, vm_ref):
    res = _dot(mem_ref[0].astype(BF16), w_ref[...]).astype(BF16)
    km_ref[0] = res[:, :QM_COLS]
    vm_ref[0] = res[:, QM_COLS:]


def _mem_kv(mem, w):
    bsz = mem.shape[0]
    out = pl.BlockSpec((1, N_MEM, QM_COLS), lambda b: (b, 0, 0))
    shape = jax.ShapeDtypeStruct((bsz, N_MEM, QM_COLS), BF16)
    return pl.pallas_call(
        _mem_kv_kernel,
        grid=(bsz,),
        in_specs=[pl.BlockSpec((1, N_MEM, D_MODEL), lambda b: (b, 0, 0)),
                  _resident((D_MODEL, 2 * QM_COLS))],
        out_specs=[out, out],
        out_shape=[shape, shape],
        compiler_params=_params(1),
        name="mem_kv",
    )(mem, w)


def _edge_variant(start, last_start):
    return jnp.where(start == 0, 0, jnp.where(start == last_start, 2, 1))


def _band_bias(slopes, q_pos, k_pos, shifts, radius, dil):
    tables = []
    for shift in shifts:
        rel = k_pos[None, :] - shift - q_pos[:, None]
        dist = jnp.abs(rel).astype(F32) * dil
        table = jnp.where((jnp.abs(rel) <= radius)[None], -LOG2E * slopes[:, None, None] * dist[None], NEG_INF)
        tables.append(table.reshape(-1, k_pos.shape[0]))
    return jnp.stack(tables)


def _attn_a_block(q4, k3, v3, bias, sink_ref, low_half):
    zero = jnp.zeros((), BF16)
    lhs = jnp.concatenate(
        [jnp.where(low_half if g == 0 else jnp.logical_not(low_half), q4[:, i * LANES:(i + 1) * LANES], zero)
         for g in range(A_KV_HEADS) for i in range(A_SLABS)], axis=0)
    s = _dot_nt(lhs, k3) + bias
    probs, totals = [], []
    for h in range(A_HEADS):
        sh = s[h * A_BLK:(h + 1) * A_BLK]
        sink = sink_ref[h]
        m = jnp.maximum(jnp.max(sh, axis=-1, keepdims=True), sink)
        p = jnp.exp2(sh - m)
        totals.append(jnp.sum(p, axis=-1, keepdims=True) + jnp.exp2(sink - m))
        probs.append(p.astype(BF16))
    o = _dot(jnp.concatenate(probs, axis=0), v3)
    slabs = []
    for i in range(A_SLABS):
        lo, hi = i, A_GROUP + i
        num = jnp.where(low_half, o[lo * A_BLK:(lo + 1) * A_BLK], o[hi * A_BLK:(hi + 1) * A_BLK])
        den = jnp.where(low_half, totals[lo], totals[hi])
        slabs.append((num / den).astype(BF16))
    return jnp.concatenate(slabs, axis=-1)


def _attn_a_bias():
    ar = jnp.arange
    return _band_bias(_alibi_slopes(A_HEADS), ar(A_BLK), ar(A_KEYS), (0, A_BLK, 2 * A_BLK), A_RADIUS, 1)


def _pair_attend(q2, k2, v2, bias, low_half):
    n = q2.shape[0]
    zero = jnp.zeros((), BF16)
    lhs = jnp.concatenate([jnp.where(low_half, q2, zero), jnp.where(low_half, zero, q2)], axis=0)
    s = _dot_nt(lhs, k2) + bias
    m = jnp.max(s, axis=-1, keepdims=True)
    p = jnp.exp2(s - m)
    denom = jnp.sum(p, axis=-1, keepdims=True)
    o = _dot(p.astype(BF16), v2)
    denom = jnp.where(low_half, denom[:n], denom[n:])
    out = jnp.where(low_half, o[:n], o[n:]) / denom
    lse = jnp.where(low_half, m[:n], m[n:]) + jnp.log(denom) * LOG2E
    return out, lse


def _attn_b_kernel(q_ref, k_ref, v_ref, qr_ref, kr_ref, vr_ref, b1_ref, b4_ref, b16_ref, o_ref, osc, lsc,
                   *, seq):
    rows = seq // MAX_DIL
    qs, ks, vs = qr_ref.at[0, 0], kr_ref.at[0, 0], vr_ref.at[0, 0]
    low_half = lax.broadcasted_iota(jnp.int32, (1, LANES), 1) < HEAD_DIM
    groups_per_tile = B_TILE // MAX_DIL

    def super_tile(st, carry):
        t0 = pl.multiple_of(st * SUPER, SUPER)
        u0 = pl.multiple_of(st * B_TILE, B_TILE)

        whole = rows == B_TILE
        keys16 = B_TILE if whole else B_KEYS
        win16 = 0 if whole else pl.multiple_of(jnp.clip(u0 - B_RADIUS, 0, rows - B_KEYS), B_RADIUS)

        def d16(r, c):
            q2 = qs[r, pl.ds(u0, B_TILE), :]
            k2 = ks[r, pl.ds(win16, keys16), :]
            v2 = vs[r, pl.ds(win16, keys16), :]
            bias = b16_ref[0, 0 if whole else _edge_variant(u0, rows - B_TILE)]
            out, lse = _pair_attend(q2, k2, v2, bias, low_half)
            osc[2, pl.ds(r, B_TILE, stride=GROUP_PITCH), :] = out
            lsc[2, pl.ds(r, B_TILE, stride=GROUP_PITCH), :] = lse
            return c

        lax.fori_loop(0, MAX_DIL, d16, 0, unroll=16)

        def d4(n, c):
            r4 = n // 4
            part = n % 4
            u1 = pl.multiple_of(u0 + part * 32, 32)
            win = pl.multiple_of(jnp.clip(u1 - 16, 0, rows - 64), 16)
            q2 = jnp.concatenate([qs[4 * a + r4, pl.ds(u1, 32), :] for a in range(4)], axis=0)
            k2 = jnp.concatenate([ks[4 * a + r4, pl.ds(win, 64), :] for a in range(4)], axis=0)
            v2 = jnp.concatenate([vs[4 * a + r4, pl.ds(win, 64), :] for a in range(4)], axis=0)
            out, lse = _pair_attend(q2, k2, v2, b4_ref[0, _edge_variant(u1, rows - 32)], low_half)
            for a in range(4):
                dst = pl.ds(part * (32 * GROUP_PITCH) + 4 * a + r4, 32, stride=GROUP_PITCH)
                osc[1, dst, :] = out[a * 32:(a + 1) * 32]
                lsc[1, dst, :] = lse[a * 32:(a + 1) * 32]
            return c

        lax.fori_loop(0, 16, d4, 0, unroll=16)

        def d1(n, c):
            off = pl.multiple_of(n * B_TILE, B_TILE)
            start = t0 + off
            win = pl.multiple_of(jnp.clip(start - B_RADIUS, 0, seq - B_KEYS), B_RADIUS)
            q2 = q_ref[0, 0, pl.ds(start, B_TILE), :]
            k2 = k_ref[0, 0, pl.ds(win, B_KEYS), :]
            v2 = v_ref[0, 0, pl.ds(win, B_KEYS), :]
            out, lse = _pair_attend(q2, k2, v2, b1_ref[0, _edge_variant(start, seq - B_TILE)], low_half)
            base = pl.multiple_of(n * (groups_per_tile * GROUP_PITCH), 8)
            for g in range(groups_per_tile):
                dst = pl.ds(base + g * GROUP_PITCH, MAX_DIL)
                osc[0, dst, :] = out[g * MAX_DIL:(g + 1) * MAX_DIL]
                lsc[0, dst, :] = lse[g * MAX_DIL:(g + 1) * MAX_DIL]
            return c

        lax.fori_loop(0, SUPER // B_TILE, d1, 0, unroll=16)

        def groups(ref, c):
            return jnp.concatenate(
                [ref[c, g * GROUP_PITCH:g * GROUP_PITCH + MAX_DIL, :] for g in range(SUPER // MAX_DIL)], axis=0)

        l0, l1, l2 = groups(lsc, 0), groups(lsc, 1), groups(lsc, 2)
        m = jnp.maximum(jnp.maximum(l0, l1), l2)
        e0, e1, e2 = jnp.exp2(l0 - m), jnp.exp2(l1 - m), jnp.exp2(l2 - m)
        mixed = (e0 * groups(osc, 0) + e1 * groups(osc, 1) + e2 * groups(osc, 2)) / (e0 + e1 + e2)
        o_ref[0, 0, pl.ds(t0, SUPER), :] = mixed.astype(BF16)
        return carry

    lax.fori_loop(0, seq // SUPER, super_tile, 0)


def _attn_b(qkv, qkv_resid, biases):
    bsz, _, seq, _ = qkv[0].shape
    rows = seq // MAX_DIL
    spec = pl.BlockSpec((1, 1, seq, LANES), lambda hp, b: (b, hp, 0, 0))
    rspec = pl.BlockSpec((1, 1, MAX_DIL, rows, LANES), lambda hp, b: (b, hp, 0, 0, 0))
    bias_specs = [pl.BlockSpec((1,) + t.shape[1:], lambda hp, b: (hp, 0, 0, 0)) for t in biases]
    return pl.pallas_call(
        functools.partial(_attn_b_kernel, seq=seq),
        grid=(B_PAIRS, bsz),
        in_specs=[spec] * 3 + [rspec] * 3 + bias_specs,
        out_specs=spec,
        out_shape=jax.ShapeDtypeStruct((bsz, B_PAIRS, seq, LANES), BF16),
        scratch_shapes=[
            pltpu.VMEM((3, SUPER // MAX_DIL * GROUP_PITCH, LANES), F32),
            pltpu.VMEM((3, SUPER // MAX_DIL * GROUP_PITCH, LANES), F32),
        ],
        compiler_params=_params(2),
        name="attn_b",
    )(*qkv, *qkv_resid, *biases)


def _attn_b_biases(seq):
    slopes = _alibi_slopes(B_HEADS)
    ar = jnp.arange
    pairs = lambda make: jnp.stack([make(slopes[2 * hp:2 * hp + 2]) for hp in range(B_PAIRS)])
    b1 = pairs(lambda sl: _band_bias(sl, ar(B_TILE), ar(B_KEYS), (0, B_RADIUS, 2 * B_RADIUS), B_RADIUS, 1))
    a4 = ar(4)[:, None]
    q4 = (4 * ar(32)[None, :] + a4).reshape(-1)
    k4 = (4 * ar(64)[None, :] + a4).reshape(-1)
    b4 = pairs(lambda sl: _band_bias(sl, q4, k4, (0, 64, 128), B_RADIUS, 4))
    if seq // MAX_DIL == B_TILE:
        b16 = pairs(lambda sl: _band_bias(sl, ar(B_TILE), ar(B_TILE), (0,), B_RADIUS, 16))
    else:
        b16 = pairs(lambda sl: _band_bias(sl, ar(B_TILE), ar(B_KEYS), (0, B_RADIUS, 2 * B_RADIUS), B_RADIUS, 16))
    return b1, b4, b16


def _merge_ln_kernel(sink_ref, x_ref, qa_ref, ka_ref, va_ref, bias_a_ref, ob_ref, km_ref, vm_ref,
                     wqm_ref, wg_ref, wbr_ref, wo_ref, g_ref, b_ref, o_ref, *, seq):
    x = x_ref[0]
    xb = x.astype(BF16)
    low_half = lax.broadcasted_iota(jnp.int32, (1, LANES), 1) < HEAD_DIM
    blocks = []
    for j in range(MERGE_ROWS // A_BLK):
        start = pl.program_id(1) * MERGE_ROWS + j * A_BLK
        window = pl.multiple_of(jnp.clip(start - A_BLK, 0, seq - A_KEYS), A_BLK)
        blocks.append(_attn_a_block(
            qa_ref[0, j * A_BLK:(j + 1) * A_BLK, :], ka_ref[0, pl.ds(window, A_KEYS), :],
            va_ref[0, pl.ds(window, A_KEYS), :], bias_a_ref[_edge_variant(start, seq - A_BLK)],
            sink_ref, low_half))
    oa = jnp.concatenate(blocks, axis=0)
    qm = _dot(xb, wqm_ref[...]).astype(BF16)
    heads = []
    for h in range(M_HEADS):
        lanes = slice(h * M_HEAD_DIM, (h + 1) * M_HEAD_DIM)
        s = _dot_nt(qm[:, lanes], km_ref[0, :, lanes]) * (LOG2E * M_HEAD_DIM ** -0.5)
        p = jnp.exp2(s - jnp.max(s, axis=-1, keepdims=True))
        denom = jnp.sum(p, axis=-1, keepdims=True)
        heads.append(_dot(p.astype(BF16), vm_ref[0, :, lanes]) / denom)
    om = jnp.concatenate(heads, axis=-1).astype(BF16)
    ob = jnp.concatenate([ob_ref[0, hp] for hp in range(B_PAIRS)], axis=-1)
    mixed = None
    for i, branch in enumerate((oa, ob, om)):
        gate = jax.nn.sigmoid(_dot(xb, wg_ref[:, i * D_MODEL:(i + 1) * D_MODEL]))
        term = gate * _dot(branch, wbr_ref[i])
        mixed = term if mixed is None else mixed + term
    y = ALPHA * x + _dot(mixed.astype(BF16), wo_ref[...])
    o_ref[0] = _layer_norm(y, g_ref[...], b_ref[...])


def _merge_ln(sink, x, qa, ka, va, bias_a, ob, km, vm, w_qm, w_gate, w_branch, w_out, g, b):
    bsz, seq, _ = x.shape
    rows = MERGE_ROWS
    tok = lambda cols: pl.BlockSpec((1, rows, cols), lambda bi, i: (bi, i, 0))
    whole = pl.BlockSpec((1, seq, LANES), lambda bi, i: (bi, 0, 0))
    mem = pl.BlockSpec((1, N_MEM, QM_COLS), lambda bi, i: (bi, 0, 0))
    return pl.pallas_call(
        functools.partial(_merge_ln_kernel, seq=seq),
        grid=(bsz, seq // rows),
        in_specs=[pl.BlockSpec(memory_space=pltpu.SMEM), tok(D_MODEL), tok(QA_COLS), whole, whole,
                  _resident(bias_a.shape),
                  pl.BlockSpec((1, B_PAIRS, rows, LANES), lambda bi, i: (bi, 0, i, 0)),
                  mem, mem,
                  _resident(w_qm.shape), _resident(w_gate.shape), _resident(w_branch.shape),
                  _resident(w_out.shape), _resident((1, D_MODEL)), _resident((1, D_MODEL))],
        out_specs=tok(D_MODEL),
        out_shape=jax.ShapeDtypeStruct((bsz, seq, D_MODEL), F32),
        compiler_params=_params(2),
        name="merge_ln",
    )(sink, x, qa, ka, va, bias_a, ob, km, vm, w_qm, w_gate, w_branch, w_out, g, b)


def _slab_order():
    cols = []
    for i in range(A_GROUP):
        cols += list(range(i * HEAD_DIM, (i + 1) * HEAD_DIM))
        cols += list(range((A_GROUP + i) * HEAD_DIM, (A_GROUP + i + 1) * HEAD_DIM))
    return jnp.asarray(cols, dtype=jnp.int32)


def kernel(x_prompt, x_sample, mem_prompt, mem_sample, ffn1_w_in, ffn1_w_out, ln1_g, ln1_b, w_in, w_mem_kv,
           sink_a, w_branch, w_out, ln2_g, ln2_b, ffn2_w_in, ffn2_w_out, ln3_g, ln3_b):
    order = _slab_order()
    bias_a = _attn_a_bias()
    q_scale = LOG2E * HEAD_DIM ** -0.5

    layers = []
    for l in range(DEPTH):
        w = w_in[l]
        qa_w = w[:, :QA_COLS][:, order] * q_scale
        kva_w = w[:, QA_COLS:QA_COLS + KVA_COLS]
        qb_w = w[:, QA_COLS + KVA_COLS:QA_COLS + KVA_COLS + BRANCH_WIDTH] * q_scale
        kvb_w = w[:, QA_COLS + KVA_COLS + BRANCH_WIDTH:QM_START]
        w_proj = jnp.concatenate([qa_w, kva_w, qb_w, kvb_w], axis=1).astype(BF16)
        w_br = jnp.concatenate([w_branch[l, :1][:, order], w_branch[l, 1:]], axis=0).astype(BF16)
        row = lambda t: t[l].reshape(1, D_MODEL).astype(F32)
        layers.append(dict(
            ffn1=(ffn1_w_in[l].astype(BF16), ffn1_w_out[l].astype(BF16), row(ln1_g), row(ln1_b)),
            ffn2=(ffn2_w_in[l].astype(BF16), ffn2_w_out[l].astype(BF16), row(ln3_g), row(ln3_b)),
            w_proj=w_proj,
            w_mem=w_mem_kv[l].astype(BF16),
            sink=sink_a[l].astype(F32) * LOG2E,
            merge=(w[:, QM_START:GATE_START].astype(BF16), w[:, GATE_START:].astype(BF16), w_br,
                   w_out[l].astype(BF16), row(ln2_g), row(ln2_b)),
        ))

    def trunk(x, mem):
        bsz, seq, _ = x.shape
        biases_b = _attn_b_biases(seq)
        for p in layers:
            x = _ffn_ln(x.reshape(bsz * seq, D_MODEL), *p["ffn1"]).reshape(bsz, seq, D_MODEL)
            qa, ka, va, *qkv_b = _in_proj(x, p["w_proj"])
            km, vm = _mem_kv(mem, p["w_mem"])
            ob = _attn_b(qkv_b[:3], qkv_b[3:], biases_b)
            x = _merge_ln(p["sink"], x, qa, ka, va, bias_a, ob, km, vm, *p["merge"])
            x = _ffn_ln(x.reshape(bsz * seq, D_MODEL), *p["ffn2"]).reshape(bsz, seq, D_MODEL)
        return x

    return trunk(x_prompt, mem_prompt), trunk(x_sample, mem_sample)
```

```python
import functools

import jax
import jax.numpy as jnp
from jax import lax
from jax.experimental import pallas as pl
from jax.experimental.pallas import tpu as pltpu

F32 = jnp.float32
BF16 = jnp.bfloat16

D_MODEL = 1024
DEPTH = 2
HEAD_DIM = 64
A_HEADS = 8
A_KV_HEADS = 2
A_GROUP = A_HEADS // A_KV_HEADS
A_RADIUS = 128
B_HEADS = 8
B_RADIUS = 64
M_HEADS = 4
M_HEAD_DIM = 128
N_MEM = 256
N_BRANCH = 3
BRANCH_WIDTH = 512
D_FF = 2816
LN_EPS = 1e-5
ALPHA = (2 * DEPTH) ** 0.25
NEG_INF = -1e30
LOG2E = 1.4426950408889634

QA_COLS = A_HEADS * HEAD_DIM
KVA_COLS = 2 * A_KV_HEADS * HEAD_DIM
QKVB_COLS = 3 * B_HEADS * HEAD_DIM
QM_COLS = M_HEADS * M_HEAD_DIM
GATE_COLS = N_BRANCH * D_MODEL
QM_START = QA_COLS + KVA_COLS + QKVB_COLS
GATE_START = QM_START + QM_COLS

LANES = 128
MAX_DIL = 16
B_PAIRS = B_HEADS * HEAD_DIM // LANES
A_SLABS = QA_COLS // LANES
PROJ_COLS = QA_COLS + KVA_COLS + QKVB_COLS

FFN_ROWS = 512
FFN_CHUNK = 256
LN_CHUNKS = 8
PROJ_ROWS = 1024
STAGE_PITCH = 72
MERGE_ROWS = 512
A_TILE = 1024
A_BLK = A_RADIUS
A_KEYS = 3 * A_BLK
B_TILE = 128
B_KEYS = 2 * B_TILE
SUPER = B_TILE * MAX_DIL
GROUP_PITCH = 24

VMEM_LIMIT = 56 * 1024 * 1024


def _resident(shape):
    zeros = (0,) * len(shape)
    return pl.BlockSpec(shape, lambda *_: zeros, pipeline_mode=pl.Buffered(1))


def _params(n_axes, semantics="parallel"):
    return pltpu.CompilerParams(dimension_semantics=(semantics,) * n_axes,
                                vmem_limit_bytes=VMEM_LIMIT)


def _layer_norm(y, g, b):
    mu = jnp.mean(y, axis=-1, keepdims=True)
    d = y - mu
    var = jnp.mean(d * d, axis=-1, keepdims=True)
    return d * lax.rsqrt(var + LN_EPS) * g + b


def _dot(a, b):
    return jnp.dot(a, b, preferred_element_type=F32)


def _dot_nt(a, b):
    return lax.dot_general(a, b, (((1,), (1,)), ((), ())), preferred_element_type=F32)


def _alibi_slopes(n):
    return 2.0 ** (-8.0 * jnp.arange(1, n + 1, dtype=F32) / n)


def _lagged_layer_norm(y_ref, g_ref, b_ref, store, n_chunks):
    @pl.when(pl.program_id(0) == 0)
    def _():
        y_ref[...] = jnp.zeros_like(y_ref)

    rows = y_ref.shape[0] // n_chunks
    zeros = []
    for c in range(n_chunks):
        out = _layer_norm(y_ref[c * rows:(c + 1) * rows, :], g_ref[...], b_ref[...])
        store(c * rows, (c + 1) * rows, out)
        bits = pltpu.bitcast(out, jnp.int32)
        bits = functools.reduce(jnp.bitwise_or, [bits[:, t * LANES:(t + 1) * LANES] for t in range(D_MODEL // LANES)])
        bits = functools.reduce(jnp.bitwise_or, [bits[r * 8:(r + 1) * 8] for r in range(rows // 8)])
        zeros.append(lax.shift_right_logical(lax.shift_right_logical(bits, 16), 16).astype(F32))
    return zeros


def _anchored(val, zero_tile):
    top = jnp.concatenate([val[0:8, 0:LANES] + zero_tile, val[0:8, LANES:]], axis=1)
    return jnp.concatenate([top, val[8:]], axis=0)


def _ffn_ln_kernel(x_ref, win_ref, wout_ref, g_ref, b_ref, o_ref, y_ref):
    def store(lo, hi, out):
        o_ref[lo:hi, :] = out

    zeros = _lagged_layer_norm(y_ref, g_ref, b_ref, store, LN_CHUNKS)
    x = x_ref[...]
    xb = x.astype(BF16)
    acc = ALPHA * x
    for c in range(D_FF // FFN_CHUNK):
        lo = c * FFN_CHUNK
        gate = _dot(xb, win_ref[:, lo:lo + FFN_CHUNK])
        up = _dot(xb, win_ref[:, D_FF + lo:D_FF + lo + FFN_CHUNK])
        if c < LN_CHUNKS:
            up = _anchored(up, zeros[c])
        act = (gate * jax.nn.sigmoid(gate) * up).astype(BF16)
        acc = acc + _dot(act, wout_ref[lo:lo + FFN_CHUNK, :])
    y_ref[...] = acc


def _ffn_ln(x2d, w_in, w_out, g, b):
    rows = x2d.shape[0]
    tiles = rows // FFN_ROWS
    return pl.pallas_call(
        _ffn_ln_kernel,
        grid=(tiles + 1,),
        in_specs=[
            pl.BlockSpec((FFN_ROWS, D_MODEL), lambda s: (jnp.minimum(s, tiles - 1), 0)),
            _resident((D_MODEL, 2 * D_FF)),
            _resident((D_FF, D_MODEL)),
            _resident((1, D_MODEL)),
            _resident((1, D_MODEL)),
        ],
        out_specs=pl.BlockSpec((FFN_ROWS, D_MODEL), lambda s: (jnp.maximum(s - 1, 0), 0)),
        out_shape=jax.ShapeDtypeStruct((rows, D_MODEL), F32),
        scratch_shapes=[pltpu.VMEM((FFN_ROWS, D_MODEL), F32)],
        compiler_params=_params(1, "arbitrary"),
        name="ffn_ln",
    )(x2d, w_in, w_out, g, b)


def _in_proj_kernel(x_ref, w_ref, qa_ref, ka_ref, va_ref, qb_ref, kb_ref, vb_ref,
                    qr_ref, kr_ref, vr_ref, stage):
    xb = x_ref[0].astype(BF16)
    base = QA_COLS + KVA_COLS
    sub = PROJ_ROWS // MAX_DIL
    for n, (ref, res_ref) in enumerate(((qb_ref, qr_ref), (kb_ref, kr_ref), (vb_ref, vr_ref))):
        lo = base + n * BRANCH_WIDTH
        res = _dot(xb, w_ref[:, lo:lo + BRANCH_WIDTH])
        for hp in range(B_PAIRS):
            slab = res[:, hp * LANES:(hp + 1) * LANES]
            ref[0, hp] = slab.astype(BF16)
            for u in range(sub):
                stage[n, hp, pl.ds(u, MAX_DIL, stride=STAGE_PITCH), :] = slab[u * MAX_DIL:(u + 1) * MAX_DIL]
            for r in range(MAX_DIL):
                res_ref[0, hp, r] = stage[n, hp, r * STAGE_PITCH:r * STAGE_PITCH + sub, :].astype(BF16)
    kva = _dot(xb, w_ref[:, QA_COLS:base]).astype(BF16)
    ka_ref[0] = kva[:, :LANES]
    va_ref[0] = kva[:, LANES:]
    qa_ref[0] = _dot(xb, w_ref[:, 0:QA_COLS]).astype(BF16)


def _in_proj(x, w):
    bsz, seq, _ = x.shape
    rows = PROJ_ROWS
    sub = rows // MAX_DIL
    tok = lambda cols: pl.BlockSpec((1, rows, cols), lambda b, i: (b, i, 0))
    pair = pl.BlockSpec((1, B_PAIRS, rows, LANES), lambda b, i: (b, 0, i, 0))
    pair_shape = jax.ShapeDtypeStruct((bsz, B_PAIRS, seq, LANES), BF16)
    resid = pl.BlockSpec((1, B_PAIRS, MAX_DIL, sub, LANES), lambda b, i: (b, 0, 0, i, 0))
    resid_shape = jax.ShapeDtypeStruct((bsz, B_PAIRS, MAX_DIL, seq // MAX_DIL, LANES), BF16)
    return pl.pallas_call(
        _in_proj_kernel,
        grid=(bsz, seq // rows),
        in_specs=[tok(D_MODEL), _resident((D_MODEL, PROJ_COLS))],
        out_specs=[tok(QA_COLS), tok(LANES), tok(LANES), pair, pair, pair, resid, resid, resid],
        out_shape=[
            jax.ShapeDtypeStruct((bsz, seq, QA_COLS), BF16),
            jax.ShapeDtypeStruct((bsz, seq, LANES), BF16),
            jax.ShapeDtypeStruct((bsz, seq, LANES), BF16),
            pair_shape, pair_shape, pair_shape, resid_shape, resid_shape, resid_shape,
        ],
        scratch_shapes=[pltpu.VMEM((3, B_PAIRS, MAX_DIL * STAGE_PITCH, LANES), F32)],
        compiler_params=_params(2),
        name="in_proj",
    )(x, w)


def _mem_kv_kernel(mem_ref, w_ref, km_ref, vm_ref):
    res = _dot(mem_ref[0].astype(BF16), w_ref[...]).astype(BF16)
    km_ref[0] = res[:, :QM_COLS]
    vm_ref[0] = res[:, QM_COLS:]


def _mem_kv(mem, w):
    bsz = mem.shape[0]
    out = pl.BlockSpec((1, N_MEM, QM_COLS), lambda b: (b, 0, 0))
    shape = jax.ShapeDtypeStruct((bsz, N_MEM, QM_COLS), BF16)
    return pl.pallas_call(
        _mem_kv_kernel,
        grid=(bsz,),
        in_specs=[pl.BlockSpec((1, N_MEM, D_MODEL), lambda b: (b, 0, 0)),
                  _resident((D_MODEL, 2 * QM_COLS))],
        out_specs=[out, out],
        out_shape=[shape, shape],
        compiler_params=_params(1),
        name="mem_kv",
    )(mem, w)


def _edge_variant(start, last_start):
    return jnp.where(start == 0, 0, jnp.where(start == last_start, 2, 1))


def _band_bias(slopes, q_pos, k_pos, shifts, radius, dil):
    tables = []
    for shift in shifts:
        rel = k_pos[None, :] - shift - q_pos[:, None]
        dist = jnp.abs(rel).astype(F32) * dil
        table = jnp.where((jnp.abs(rel) <= radius)[None], -LOG2E * slopes[:, None, None] * dist[None], NEG_INF)
        tables.append(table.reshape(-1, k_pos.shape[0]))
    return jnp.stack(tables)


def _attn_a_block(q4, k3, v3, bias, sink_ref, low_half):
    zero = jnp.zeros((), BF16)
    lhs = jnp.concatenate(
        [jnp.where(low_half if g == 0 else jnp.logical_not(low_half), q4[:, i * LANES:(i + 1) * LANES], zero)
         for g in range(A_KV_HEADS) for i in range(A_SLABS)], axis=0)
    s = _dot_nt(lhs, k3) + bias
    probs, totals = [], []
    for h in range(A_HEADS):
        sh = s[h * A_BLK:(h + 1) * A_BLK]
        sink = sink_ref[h]
        m = jnp.maximum(jnp.max(sh, axis=-1, keepdims=True), sink)
        p = jnp.exp2(sh - m)
        totals.append(jnp.sum(p, axis=-1, keepdims=True) + jnp.exp2(sink - m))
        probs.append(p.astype(BF16))
    o = _dot(jnp.concatenate(probs, axis=0), v3)
    slabs = []
    for i in range(A_SLABS):
        lo, hi = i, A_GROUP + i
        num = jnp.where(low_half, o[lo * A_BLK:(lo + 1) * A_BLK], o[hi * A_BLK:(hi + 1) * A_BLK])
        den = jnp.where(low_half, totals[lo], totals[hi])
        slabs.append((num / den).astype(BF16))
    return jnp.concatenate(slabs, axis=-1)


def _attn_a_bias():
    ar = jnp.arange
    return _band_bias(_alibi_slopes(A_HEADS), ar(A_BLK), ar(A_KEYS), (0, A_BLK, 2 * A_BLK), A_RADIUS, 1)


def _pair_attend(q2, k2, v2, bias, low_half):
    n = q2.shape[0]
    zero = jnp.zeros((), BF16)
    lhs = jnp.concatenate([jnp.where(low_half, q2, zero), jnp.where(low_half, zero, q2)], axis=0)
    s = _dot_nt(lhs, k2) + bias
    m = jnp.max(s, axis=-1, keepdims=True)
    p = jnp.exp2(s - m)
    denom = jnp.sum(p, axis=-1, keepdims=True)
    o = _dot(p.astype(BF16), v2)
    denom = jnp.where(low_half, denom[:n], denom[n:])
    out = jnp.where(low_half, o[:n], o[n:]) / denom
    lse = jnp.where(low_half, m[:n], m[n:]) + jnp.log(denom) * LOG2E
    return out, lse


def _attn_b_kernel(q_ref, k_ref, v_ref, qr_ref, kr_ref, vr_ref, b1_ref, b4_ref, b16_ref, o_ref, osc, lsc,
                   *, seq):
    rows = seq // MAX_DIL
    qs, ks, vs = qr_ref.at[0, 0], kr_ref.at[0, 0], vr_ref.at[0, 0]
    low_half = lax.broadcasted_iota(jnp.int32, (1, LANES), 1) < HEAD_DIM
    groups_per_tile = B_TILE // MAX_DIL

    def super_tile(st, carry):
        t0 = pl.multiple_of(st * SUPER, SUPER)
        u0 = pl.multiple_of(st * B_TILE, B_TILE)

        whole = rows == B_TILE
        keys16 = B_TILE if whole else B_KEYS
        win16 = 0 if whole else pl.multiple_of(jnp.clip(u0 - B_RADIUS, 0, rows - B_KEYS), B_RADIUS)

        def d16(r, c):
            q2 = qs[r, pl.ds(u0, B_TILE), :]
            k2 = ks[r, pl.ds(win16, keys16), :]
            v2 = vs[r, pl.ds(win16, keys16), :]
            bias = b16_ref[0, 0 if whole else _edge_variant(u0, rows - B_TILE)]
            out, lse = _pair_attend(q2, k2, v2, bias, low_half)
            osc[2, pl.ds(r, B_TILE, stride=GROUP_PITCH), :] = out
            lsc[2, pl.ds(r, B_TILE, stride=GROUP_PITCH), :] = lse
            return c

        lax.fori_loop(0, MAX_DIL, d16, 0, unroll=16)

        def d4(n, c):
            r4 = n // 4
            part = n % 4
            u1 = pl.multiple_of(u0 + part * 32, 32)
            win = pl.multiple_of(jnp.clip(u1 - 16, 0, rows - 64), 16)
            q2 = jnp.concatenate([qs[4 * a + r4, pl.ds(u1, 32), :] for a in range(4)], axis=0)
            k2 = jnp.concatenate([ks[4 * a + r4, pl.ds(win, 64), :] for a in range(4)], axis=0)
            v2 = jnp.concatenate([vs[4 * a + r4, pl.ds(win, 64), :] for a in range(4)], axis=0)
            out, lse = _pair_attend(q2, k2, v2, b4_ref[0, _edge_variant(u1, rows - 32)], low_half)
            for a in range(4):
                dst = pl.ds(part * (32 * GROUP_PITCH) + 4 * a + r4, 32, stride=GROUP_PITCH)
                osc[1, dst, :] = out[a * 32:(a + 1) * 32]
                lsc[1, dst, :] = lse[a * 32:(a + 1) * 32]
            return c

        lax.fori_loop(0, 16, d4, 0, unroll=16)

        def d1(n, c):
            off = pl.multiple_of(n * B_TILE, B_TILE)
            start = t0 + off
            win = pl.multiple_of(jnp.clip(start - B_RADIUS, 0, seq - B_KEYS), B_RADIUS)
            q2 = q_ref[0, 0, pl.ds(start, B_TILE), :]
            k2 = k_ref[0, 0, pl.ds(win, B_KEYS), :]
            v2 = v_ref[0, 0, pl.ds(win, B_KEYS), :]
            out, lse = _pair_attend(q2, k2, v2, b1_ref[0, _edge_variant(start, seq - B_TILE)], low_half)
            base = pl.multiple_of(n * (groups_per_tile * GROUP_PITCH), 8)
            for g in range(groups_per_tile):
                dst = pl.ds(base + g * GROUP_PITCH, MAX_DIL)
                osc[0, dst, :] = out[g * MAX_DIL:(g + 1) * MAX_DIL]
                lsc[0, dst, :] = lse[g * MAX_DIL:(g + 1) * MAX_DIL]
            return c

        lax.fori_loop(0, SUPER // B_TILE, d1, 0, unroll=16)

        def groups(ref, c):
            return jnp.concatenate(
                [ref[c, g * GROUP_PITCH:g * GROUP_PITCH + MAX_DIL, :] for g in range(SUPER // MAX_DIL)], axis=0)

        l0, l1, l2 = groups(lsc, 0), groups(lsc, 1), groups(lsc, 2)
        m = jnp.maximum(jnp.maximum(l0, l1), l2)
        e0, e1, e2 = jnp.exp2(l0 - m), jnp.exp2(l1 - m), jnp.exp2(l2 - m)
        mixed = (e0 * groups(osc, 0) + e1 * groups(osc, 1) + e2 * groups(osc, 2)) / (e0 + e1 + e2)
        o_ref[0, 0, pl.ds(t0, SUPER), :] = mixed.astype(BF16)
        return carry

    lax.fori_loop(0, seq // SUPER, super_tile, 0)


def _attn_b(qkv, qkv_resid, biases):
    bsz, _, seq, _ = qkv[0].shape
    rows = seq // MAX_DIL
    spec = pl.BlockSpec((1, 1, seq, LANES), lambda hp, b: (b, hp, 0, 0))
    rspec = pl.BlockSpec((1, 1, MAX_DIL, rows, LANES), lambda hp, b: (b, hp, 0, 0, 0))
    bias_specs = [pl.BlockSpec((1,) + t.shape[1:], lambda hp, b: (hp, 0, 0, 0)) for t in biases]
    return pl.pallas_call(
        functools.partial(_attn_b_kernel, seq=seq),
        grid=(B_PAIRS, bsz),
        in_specs=[spec] * 3 + [rspec] * 3 + bias_specs,
        out_specs=spec,
        out_shape=jax.ShapeDtypeStruct((bsz, B_PAIRS, seq, LANES), BF16),
        scratch_shapes=[
            pltpu.VMEM((3, SUPER // MAX_DIL * GROUP_PITCH, LANES), F32),
            pltpu.VMEM((3, SUPER // MAX_DIL * GROUP_PITCH, LANES), F32),
        ],
        compiler_params=_params(2),
        name="attn_b",
    )(*qkv, *qkv_resid, *biases)


def _attn_b_biases(seq):
    slopes = _alibi_slopes(B_HEADS)
    ar = jnp.arange
    pairs = lambda make: jnp.stack([make(slopes[2 * hp:2 * hp + 2]) for hp in range(B_PAIRS)])
    b1 = pairs(lambda sl: _band_bias(sl, ar(B_TILE), ar(B_KEYS), (0, B_RADIUS, 2 * B_RADIUS), B_RADIUS, 1))
    a4 = ar(4)[:, None]
    q4 = (4 * ar(32)[None, :] + a4).reshape(-1)
    k4 = (4 * ar(64)[None, :] + a4).reshape(-1)
    b4 = pairs(lambda sl: _band_bias(sl, q4, k4, (0, 64, 128), B_RADIUS, 4))
    if seq // MAX_DIL == B_TILE:
        b16 = pairs(lambda sl: _band_bias(sl, ar(B_TILE), ar(B_TILE), (0,), B_RADIUS, 16))
    else:
        b16 = pairs(lambda sl: _band_bias(sl, ar(B_TILE), ar(B_KEYS), (0, B_RADIUS, 2 * B_RADIUS), B_RADIUS, 16))
    return b1, b4, b16


def _merge_ln_kernel(sink_ref, x_ref, qa_ref, ka_ref, va_ref, bias_a_ref, ob_ref, km_ref, vm_ref,
                     wqm_ref, wg_ref, wbr_ref, wo_ref, g_ref, b_ref, o_ref, *, seq):
    x = x_ref[0]
    xb = x.astype(BF16)
    low_half = lax.broadcasted_iota(jnp.int32, (1, LANES), 1) < HEAD_DIM
    blocks = []
    for j in range(MERGE_ROWS // A_BLK):
        start = pl.program_id(1) * MERGE_ROWS + j * A_BLK
        window = pl.multiple_of(jnp.clip(start - A_BLK, 0, seq - A_KEYS), A_BLK)
        blocks.append(_attn_a_block(
            qa_ref[0, j * A_BLK:(j + 1) * A_BLK, :], ka_ref[0, pl.ds(window, A_KEYS), :],
            va_ref[0, pl.ds(window, A_KEYS), :], bias_a_ref[_edge_variant(start, seq - A_BLK)],
            sink_ref, low_half))
    oa = jnp.concatenate(blocks, axis=0)
    qm = _dot(xb, wqm_ref[...]).astype(BF16)
    heads = []
    for h in range(M_HEADS):
        lanes = slice(h * M_HEAD_DIM, (h + 1) * M_HEAD_DIM)
        s = _dot_nt(qm[:, lanes], km_ref[0, :, lanes]) * (LOG2E * M_HEAD_DIM ** -0.5)
        p = jnp.exp2(s - jnp.max(s, axis=-1, keepdims=True))
        denom = jnp.sum(p, axis=-1, keepdims=True)
        heads.append(_dot(p.astype(BF16), vm_ref[0, :, lanes]) / denom)
    om = jnp.concatenate(heads, axis=-1).astype(BF16)
    ob = jnp.concatenate([ob_ref[0, hp] for hp in range(B_PAIRS)], axis=-1)
    mixed = None
    for i, branch in enumerate((oa, ob, om)):
        gate = jax.nn.sigmoid(_dot(xb, wg_ref[:, i * D_MODEL:(i + 1) * D_MODEL]))
        term = gate * _dot(branch, wbr_ref[i])
        mixed = term if mixed is None else mixed + term
    y = ALPHA * x + _dot(mixed.astype(BF16), wo_ref[...])
    o_ref[0] = _layer_norm(y, g_ref[...], b_ref[...])


def _merge_ln(sink, x, qa, ka, va, bias_a, ob, km, vm, w_qm, w_gate, w_branch, w_out, g, b):
    bsz, seq, _ = x.shape
    rows = MERGE_ROWS
    tok = lambda cols: pl.BlockSpec((1, rows, cols), lambda bi, i: (bi, i, 0))
    whole = pl.BlockSpec((1, seq, LANES), lambda bi, i: (bi, 0, 0))
    mem = pl.BlockSpec((1, N_MEM, QM_COLS), lambda bi, i: (bi, 0, 0))
    return pl.pallas_call(
        functools.partial(_merge_ln_kernel, seq=seq),
        grid=(bsz, seq // rows),
        in_specs=[pl.BlockSpec(memory_space=pltpu.SMEM), tok(D_MODEL), tok(QA_COLS), whole, whole,
                  _resident(bias_a.shape),
                  pl.BlockSpec((1, B_PAIRS, rows, LANES), lambda bi, i: (bi, 0, i, 0)),
                  mem, mem,
                  _resident(w_qm.shape), _resident(w_gate.shape), _resident(w_branch.shape),
                  _resident(w_out.shape), _resident((1, D_MODEL)), _resident((1, D_MODEL))],
        out_specs=tok(D_MODEL),
        out_shape=jax.ShapeDtypeStruct((bsz, seq, D_MODEL), F32),
        compiler_params=_params(2),
        name="merge_ln",
    )(sink, x, qa, ka, va, bias_a, ob, km, vm, w_qm, w_gate, w_branch, w_out, g, b)


def _slab_order():
    cols = []
    for i in range(A_GROUP):
        cols += list(range(i * HEAD_DIM, (i + 1) * HEAD_DIM))
        cols += list(range((A_GROUP + i) * HEAD_DIM, (A_GROUP + i + 1) * HEAD_DIM))
    return jnp.asarray(cols, dtype=jnp.int32)


def kernel(x_prompt, x_sample, mem_prompt, mem_sample, ffn1_w_in, ffn1_w_out, ln1_g, ln1_b, w_in, w_mem_kv,
           sink_a, w_branch, w_out, ln2_g, ln2_b, ffn2_w_in, ffn2_w_out, ln3_g, ln3_b):
    order = _slab_order()
    bias_a = _attn_a_bias()
    q_scale = LOG2E * HEAD_DIM ** -0.5

    layers = []
    for l in range(DEPTH):
        w = w_in[l]
        qa_w = w[:, :QA_COLS][:, order] * q_scale
        kva_w = w[:, QA_COLS:QA_COLS + KVA_COLS]
        qb_w = w[:, QA_COLS + KVA_COLS:QA_COLS + KVA_COLS + BRANCH_WIDTH] * q_scale
        kvb_w = w[:, QA_COLS + KVA_COLS + BRANCH_WIDTH:QM_START]
        w_proj = jnp.concatenate([qa_w, kva_w, qb_w, kvb_w], axis=1).astype(BF16)
        w_br = jnp.concatenate([w_branch[l, :1][:, order], w_branch[l, 1:]], axis=0).astype(BF16)
        row = lambda t: t[l].reshape(1, D_MODEL).astype(F32)
        layers.append(dict(
            ffn1=(ffn1_w_in[l].astype(BF16), (0.5 * ffn1_w_out[l]).astype(BF16), row(ln1_g), row(ln1_b)),
            ffn2=(ffn2_w_in[l].astype(BF16), (0.5 * ffn2_w_out[l]).astype(BF16), row(ln3_g), row(ln3_b)),
            w_proj=w_proj,
            w_mem=w_mem_kv[l].astype(BF16),
            sink=sink_a[l].astype(F32) * LOG2E,
            merge=(w[:, QM_START:GATE_START].astype(BF16), w[:, GATE_START:].astype(BF16), w_br,
                   w_out[l].astype(BF16), row(ln2_g), row(ln2_b)),
        ))

    def trunk(x, mem):
        bsz, seq, _ = x.shape
        biases_b = _attn_b_biases(seq)
        for p in layers:
            x = _ffn_ln(x.reshape(bsz * seq, D_MODEL), *p["ffn1"]).reshape(bsz, seq, D_MODEL)
            qa, ka, va, *qkv_b = _in_proj(x, p["w_proj"])
            km, vm = _mem_kv(mem, p["w_mem"])
            ob = _attn_b(qkv_b[:3], qkv_b[3:], biases_b)
            x = _merge_ln(p["sink"], x, qa, ka, va, bias_a, ob, km, vm, *p["merge"])
            x = _ffn_ln(x.reshape(bsz * seq, D_MODEL), *p["ffn2"]).reshape(bsz, seq, D_MODEL)
        return x

    return trunk(x_prompt, mem_prompt), trunk(x_sample, mem_sample)
```

```python
import functools

import jax
import jax.numpy as jnp
from jax import lax
from jax.experimental import pallas as pl
from jax.experimental.pallas import tpu as pltpu

F32 = jnp.float32
BF16 = jnp.bfloat16

D_MODEL = 1024
DEPTH = 2
HEAD_DIM = 64
A_HEADS = 8
A_KV_HEADS = 2
A_GROUP = A_HEADS // A_KV_HEADS
A_RADIUS = 128
B_HEADS = 8
B_RADIUS = 64
M_HEADS = 4
M_HEAD_DIM = 128
N_MEM = 256
N_BRANCH = 3
BRANCH_WIDTH = 512
D_FF = 2816
LN_EPS = 1e-5
ALPHA = (2 * DEPTH) ** 0.25
NEG_INF = -1e30
LOG2E = 1.4426950408889634

QA_COLS = A_HEADS * HEAD_DIM
KVA_COLS = 2 * A_KV_HEADS * HEAD_DIM
QKVB_COLS = 3 * B_HEADS * HEAD_DIM
QM_COLS = M_HEADS * M_HEAD_DIM
GATE_COLS = N_BRANCH * D_MODEL
QM_START = QA_COLS + KVA_COLS + QKVB_COLS
GATE_START = QM_START + QM_COLS

LANES = 128
MAX_DIL = 16
B_PAIRS = B_HEADS * HEAD_DIM // LANES
A_SLABS = QA_COLS // LANES
PROJ_COLS = QA_COLS + KVA_COLS + QKVB_COLS

FFN_ROWS = 512
FFN_CHUNK = 256
LN_CHUNKS = 8
PROJ_ROWS = 1024
STAGE_PITCH = 72
MERGE_ROWS = 512
A_TILE = 1024
A_BLK = A_RADIUS
A_KEYS = 3 * A_BLK
B_TILE = 128
B_KEYS = 2 * B_TILE
SUPER = B_TILE * MAX_DIL
GROUP_PITCH = 24

VMEM_LIMIT = 56 * 1024 * 1024


def _resident(shape):
    zeros = (0,) * len(shape)
    return pl.BlockSpec(shape, lambda *_: zeros, pipeline_mode=pl.Buffered(1))


def _params(n_axes, semantics="parallel"):
    return pltpu.CompilerParams(dimension_semantics=(semantics,) * n_axes,
                                vmem_limit_bytes=VMEM_LIMIT)


def _layer_norm(y, g, b):
    mu = jnp.mean(y, axis=-1, keepdims=True)
    d = y - mu
    var = jnp.mean(d * d, axis=-1, keepdims=True)
    return d * lax.rsqrt(var + LN_EPS) * g + b


def _dot(a, b):
    return jnp.dot(a, b, preferred_element_type=F32)


def _dot_nt(a, b):
    return lax.dot_general(a, b, (((1,), (1,)), ((), ())), preferred_element_type=F32)


def _alibi_slopes(n):
    return 2.0 ** (-8.0 * jnp.arange(1, n + 1, dtype=F32) / n)


def _lagged_layer_norm(y_ref, g_ref, b_ref, store, n_chunks):
    @pl.when(pl.program_id(0) == 0)
    def _():
        y_ref[...] = jnp.zeros_like(y_ref)

    rows = y_ref.shape[0] // n_chunks
    zeros = []
    for c in range(n_chunks):
        out = _layer_norm(y_ref[c * rows:(c + 1) * rows, :], g_ref[...], b_ref[...])
        store(c * rows, (c + 1) * rows, out)
        bits = pltpu.bitcast(out, jnp.int32)
        bits = functools.reduce(jnp.bitwise_or, [bits[:, t * LANES:(t + 1) * LANES] for t in range(D_MODEL // LANES)])
        bits = functools.reduce(jnp.bitwise_or, [bits[r * 8:(r + 1) * 8] for r in range(rows // 8)])
        zeros.append(lax.shift_right_logical(lax.shift_right_logical(bits, 16), 16).astype(F32))
    return zeros


def _anchored(val, zero_tile):
    top = jnp.concatenate([val[0:8, 0:LANES] + zero_tile, val[0:8, LANES:]], axis=1)
    return jnp.concatenate([top, val[8:]], axis=0)


def _ffn_ln_kernel(x_ref, win_ref, wout_ref, g_ref, b_ref, *rest):
    o_ref, y_ref = rest[0], rest[-1]
    half_ref = rest[1] if len(rest) == 3 else None

    def store(lo, hi, out):
        o_ref[lo:hi, :] = out
        if half_ref is not None:
            half_ref[lo:hi, :] = out.astype(BF16)

    zeros = _lagged_layer_norm(y_ref, g_ref, b_ref, store, LN_CHUNKS)
    x = x_ref[...]
    xb = x.astype(BF16)
    acc = ALPHA * x
    for c in range(D_FF // FFN_CHUNK):
        lo = c * FFN_CHUNK
        gate = _dot(xb, win_ref[:, lo:lo + FFN_CHUNK])
        up = _dot(xb, win_ref[:, D_FF + lo:D_FF + lo + FFN_CHUNK])
        if c < LN_CHUNKS:
            up = _anchored(up, zeros[c])
        act = (gate * jax.nn.sigmoid(gate) * up).astype(BF16)
        acc = acc + _dot(act, wout_ref[lo:lo + FFN_CHUNK, :])
    y_ref[...] = acc


def _ffn_ln(x2d, w_in, w_out, g, b, with_bf16_copy=False):
    rows = x2d.shape[0]
    tiles = rows // FFN_ROWS
    out_spec = pl.BlockSpec((FFN_ROWS, D_MODEL), lambda s: (jnp.maximum(s - 1, 0), 0))
    out_specs, out_shape = [out_spec], [jax.ShapeDtypeStruct((rows, D_MODEL), F32)]
    if with_bf16_copy:
        out_specs.append(out_spec)
        out_shape.append(jax.ShapeDtypeStruct((rows, D_MODEL), BF16))
    outs = pl.pallas_call(
        _ffn_ln_kernel,
        grid=(tiles + 1,),
        in_specs=[
            pl.BlockSpec((FFN_ROWS, D_MODEL), lambda s: (jnp.minimum(s, tiles - 1), 0)),
            _resident((D_MODEL, 2 * D_FF)),
            _resident((D_FF, D_MODEL)),
            _resident((1, D_MODEL)),
            _resident((1, D_MODEL)),
        ],
        out_specs=out_specs,
        out_shape=out_shape,
        scratch_shapes=[pltpu.VMEM((FFN_ROWS, D_MODEL), F32)],
        compiler_params=_params(1, "arbitrary"),
        name="ffn_ln",
    )(x2d, w_in, w_out, g, b)
    return outs if with_bf16_copy else outs[0]


def _in_proj_kernel(x_ref, w_ref, qa_ref, ka_ref, va_ref, qb_ref, kb_ref, vb_ref,
                    qr_ref, kr_ref, vr_ref, stage):
    xb = x_ref[0]
    base = QA_COLS + KVA_COLS
    sub = PROJ_ROWS // MAX_DIL
    for n, (ref, res_ref) in enumerate(((qb_ref, qr_ref), (kb_ref, kr_ref), (vb_ref, vr_ref))):
        lo = base + n * BRANCH_WIDTH
        res = _dot(xb, w_ref[:, lo:lo + BRANCH_WIDTH])
        for hp in range(B_PAIRS):
            slab = res[:, hp * LANES:(hp + 1) * LANES]
            ref[0, hp] = slab.astype(BF16)
            for u in range(sub):
                stage[n, hp, pl.ds(u, MAX_DIL, stride=STAGE_PITCH), :] = slab[u * MAX_DIL:(u + 1) * MAX_DIL]
            for r in range(MAX_DIL):
                res_ref[0, hp, r] = stage[n, hp, r * STAGE_PITCH:r * STAGE_PITCH + sub, :].astype(BF16)
    kva = _dot(xb, w_ref[:, QA_COLS:base]).astype(BF16)
    ka_ref[0] = kva[:, :LANES]
    va_ref[0] = kva[:, LANES:]
    qa_ref[0] = _dot(xb, w_ref[:, 0:QA_COLS]).astype(BF16)


def _in_proj(x, w):
    bsz, seq, _ = x.shape
    rows = PROJ_ROWS
    sub = rows // MAX_DIL
    tok = lambda cols: pl.BlockSpec((1, rows, cols), lambda b, i: (b, i, 0))
    pair = pl.BlockSpec((1, B_PAIRS, rows, LANES), lambda b, i: (b, 0, i, 0))
    pair_shape = jax.ShapeDtypeStruct((bsz, B_PAIRS, seq, LANES), BF16)
    resid = pl.BlockSpec((1, B_PAIRS, MAX_DIL, sub, LANES), lambda b, i: (b, 0, 0, i, 0))
    resid_shape = jax.ShapeDtypeStruct((bsz, B_PAIRS, MAX_DIL, seq // MAX_DIL, LANES), BF16)
    return pl.pallas_call(
        _in_proj_kernel,
        grid=(bsz, seq // rows),
        in_specs=[tok(D_MODEL), _resident((D_MODEL, PROJ_COLS))],
        out_specs=[tok(QA_COLS), tok(LANES), tok(LANES), pair, pair, pair, resid, resid, resid],
        out_shape=[
            jax.ShapeDtypeStruct((bsz, seq, QA_COLS), BF16),
            jax.ShapeDtypeStruct((bsz, seq, LANES), BF16),
            jax.ShapeDtypeStruct((bsz, seq, LANES), BF16),
            pair_shape, pair_shape, pair_shape, resid_shape, resid_shape, resid_shape,
        ],
        scratch_shapes=[pltpu.VMEM((3, B_PAIRS, MAX_DIL * STAGE_PITCH, LANES), F32)],
        compiler_params=_params(2),
        name="in_proj",
    )(x, w)


def _mem_kv_kernel(mem_ref, w_ref, km_ref, vm_ref):
    res = _dot(mem_ref[0].astype(BF16), w_ref[...]).astype(BF16)
    km_ref[0] = res[:, :QM_COLS]
    vm_ref[0] = res[:, QM_COLS:]


def _mem_kv(mem, w):
    bsz = mem.shape[0]
    out = pl.BlockSpec((1, N_MEM, QM_COLS), lambda b: (b, 0, 0))
    shape = jax.ShapeDtypeStruct((bsz, N_MEM, QM_COLS), BF16)
    return pl.pallas_call(
        _mem_kv_kernel,
        grid=(bsz,),
        in_specs=[pl.BlockSpec((1, N_MEM, D_MODEL), lambda b: (b, 0, 0)),
                  _resident((D_MODEL, 2 * QM_COLS))],
        out_specs=[out, out],
        out_shape=[shape, shape],
        compiler_params=_params(1),
        name="mem_kv",
    )(mem, w)


def _edge_variant(start, last_start):
    return jnp.where(start == 0, 0, jnp.where(start == last_start, 2, 1))


def _band_bias(slopes, q_pos, k_pos, shifts, radius, dil):
    tables = []
    for shift in shifts:
        rel = k_pos[None, :] - shift - q_pos[:, None]
        dist = jnp.abs(rel).astype(F32) * dil
        table = jnp.where((jnp.abs(rel) <= radius)[None], -LOG2E * slopes[:, None, None] * dist[None], NEG_INF)
        tables.append(table.reshape(-1, k_pos.shape[0]))
    return jnp.stack(tables)


def _attn_a_block(q4, k3, v3, bias, sink_ref, low_half):
    zero = jnp.zeros((), BF16)
    lhs = jnp.concatenate(
        [jnp.where(low_half if g == 0 else jnp.logical_not(low_half), q4[:, i * LANES:(i + 1) * LANES], zero)
         for g in range(A_KV_HEADS) for i in range(A_SLABS)], axis=0)
    s = _dot_nt(lhs, k3) + bias
    probs, totals = [], []
    for h in range(A_HEADS):
        sh = s[h * A_BLK:(h + 1) * A_BLK]
        sink = sink_ref[h]
        m = jnp.maximum(jnp.max(sh, axis=-1, keepdims=True), sink)
        p = jnp.exp2(sh - m)
        totals.append(jnp.sum(p, axis=-1, keepdims=True) + jnp.exp2(sink - m))
        probs.append(p.astype(BF16))
    o = _dot(jnp.concatenate(probs, axis=0), v3)
    slabs = []
    for i in range(A_SLABS):
        lo, hi = i, A_GROUP + i
        num = jnp.where(low_half, o[lo * A_BLK:(lo + 1) * A_BLK], o[hi * A_BLK:(hi + 1) * A_BLK])
        den = jnp.where(low_half, totals[lo], totals[hi])
        slabs.append((num / den).astype(BF16))
    return jnp.concatenate(slabs, axis=-1)


def _attn_a_bias():
    ar = jnp.arange
    return _band_bias(_alibi_slopes(A_HEADS), ar(A_BLK), ar(A_KEYS), (0, A_BLK, 2 * A_BLK), A_RADIUS, 1)


def _pair_attend(q2, k2, v2, bias, low_half):
    n = q2.shape[0]
    zero = jnp.zeros((), BF16)
    lhs = jnp.concatenate([jnp.where(low_half, q2, zero), jnp.where(low_half, zero, q2)], axis=0)
    s = _dot_nt(lhs, k2) + bias
    m = jnp.max(s, axis=-1, keepdims=True)
    p = jnp.exp2(s - m)
    denom = jnp.sum(p, axis=-1, keepdims=True)
    o = _dot(p.astype(BF16), v2)
    denom = jnp.where(low_half, denom[:n], denom[n:])
    out = jnp.where(low_half, o[:n], o[n:]) / denom
    lse = jnp.where(low_half, m[:n], m[n:]) + jnp.log(denom) * LOG2E
    return out, lse


def _attn_b_kernel(q_ref, k_ref, v_ref, qr_ref, kr_ref, vr_ref, b1_ref, b4_ref, b16_ref, o_ref, osc, lsc,
                   *, seq):
    rows = seq // MAX_DIL
    qs, ks, vs = qr_ref.at[0, 0], kr_ref.at[0, 0], vr_ref.at[0, 0]
    low_half = lax.broadcasted_iota(jnp.int32, (1, LANES), 1) < HEAD_DIM
    groups_per_tile = B_TILE // MAX_DIL

    def super_tile(st, carry):
        t0 = pl.multiple_of(st * SUPER, SUPER)
        u0 = pl.multiple_of(st * B_TILE, B_TILE)

        whole = rows == B_TILE
        keys16 = B_TILE if whole else B_KEYS
        win16 = 0 if whole else pl.multiple_of(jnp.clip(u0 - B_RADIUS, 0, rows - B_KEYS), B_RADIUS)

        def d16(r, c):
            q2 = qs[r, pl.ds(u0, B_TILE), :]
            k2 = ks[r, pl.ds(win16, keys16), :]
            v2 = vs[r, pl.ds(win16, keys16), :]
            bias = b16_ref[0, 0 if whole else _edge_variant(u0, rows - B_TILE)]
            out, lse = _pair_attend(q2, k2, v2, bias, low_half)
            osc[2, pl.ds(r, B_TILE, stride=GROUP_PITCH), :] = out
            lsc[2, pl.ds(r, B_TILE, stride=GROUP_PITCH), :] = lse
            return c

        lax.fori_loop(0, MAX_DIL, d16, 0, unroll=16)

        def d4(n, c):
            r4 = n // 4
            part = n % 4
            u1 = pl.multiple_of(u0 + part * 32, 32)
            win = pl.multiple_of(jnp.clip(u1 - 16, 0, rows - 64), 16)
            q2 = jnp.concatenate([qs[4 * a + r4, pl.ds(u1, 32), :] for a in range(4)], axis=0)
            k2 = jnp.concatenate([ks[4 * a + r4, pl.ds(win, 64), :] for a in range(4)], axis=0)
            v2 = jnp.concatenate([vs[4 * a + r4, pl.ds(win, 64), :] for a in range(4)], axis=0)
            out, lse = _pair_attend(q2, k2, v2, b4_ref[0, _edge_variant(u1, rows - 32)], low_half)
            for a in range(4):
                dst = pl.ds(part * (32 * GROUP_PITCH) + 4 * a + r4, 32, stride=GROUP_PITCH)
                osc[1, dst, :] = out[a * 32:(a + 1) * 32]
                lsc[1, dst, :] = lse[a * 32:(a + 1) * 32]
            return c

        lax.fori_loop(0, 16, d4, 0, unroll=16)

        def d1(n, c):
            off = pl.multiple_of(n * B_TILE, B_TILE)
            start = t0 + off
            win = pl.multiple_of(jnp.clip(start - B_RADIUS, 0, seq - B_KEYS), B_RADIUS)
            q2 = q_ref[0, 0, pl.ds(start, B_TILE), :]
            k2 = k_ref[0, 0, pl.ds(win, B_KEYS), :]
            v2 = v_ref[0, 0, pl.ds(win, B_KEYS), :]
            out, lse = _pair_attend(q2, k2, v2, b1_ref[0, _edge_variant(start, seq - B_TILE)], low_half)
            base = pl.multiple_of(n * (groups_per_tile * GROUP_PITCH), 8)
            for g in range(groups_per_tile):
                dst = pl.ds(base + g * GROUP_PITCH, MAX_DIL)
                osc[0, dst, :] = out[g * MAX_DIL:(g + 1) * MAX_DIL]
                lsc[0, dst, :] = lse[g * MAX_DIL:(g + 1) * MAX_DIL]
            return c

        lax.fori_loop(0, SUPER // B_TILE, d1, 0, unroll=16)

        def groups(ref, c):
            return jnp.concatenate(
                [ref[c, g * GROUP_PITCH:g * GROUP_PITCH + MAX_DIL, :] for g in range(SUPER // MAX_DIL)], axis=0)

        l0, l1, l2 = groups(lsc, 0), groups(lsc, 1), groups(lsc, 2)
        m = jnp.maximum(jnp.maximum(l0, l1), l2)
        e0, e1, e2 = jnp.exp2(l0 - m), jnp.exp2(l1 - m), jnp.exp2(l2 - m)
        mixed = (e0 * groups(osc, 0) + e1 * groups(osc, 1) + e2 * groups(osc, 2)) / (e0 + e1 + e2)
        o_ref[0, 0, pl.ds(t0, SUPER), :] = mixed.astype(BF16)
        return carry

    lax.fori_loop(0, seq // SUPER, super_tile, 0)


def _attn_b(qkv, qkv_resid, biases):
    bsz, _, seq, _ = qkv[0].shape
    rows = seq // MAX_DIL
    spec = pl.BlockSpec((1, 1, seq, LANES), lambda hp, b: (b, hp, 0, 0))
    rspec = pl.BlockSpec((1, 1, MAX_DIL, rows, LANES), lambda hp, b: (b, hp, 0, 0, 0))
    bias_specs = [pl.BlockSpec((1,) + t.shape[1:], lambda hp, b: (hp, 0, 0, 0)) for t in biases]
    return pl.pallas_call(
        functools.partial(_attn_b_kernel, seq=seq),
        grid=(B_PAIRS, bsz),
        in_specs=[spec] * 3 + [rspec] * 3 + bias_specs,
        out_specs=spec,
        out_shape=jax.ShapeDtypeStruct((bsz, B_PAIRS, seq, LANES), BF16),
        scratch_shapes=[
            pltpu.VMEM((3, SUPER // MAX_DIL * GROUP_PITCH, LANES), F32),
            pltpu.VMEM((3, SUPER // MAX_DIL * GROUP_PITCH, LANES), F32),
        ],
        compiler_params=_params(2),
        name="attn_b",
    )(*qkv, *qkv_resid, *biases)


def _attn_b_biases(seq):
    slopes = _alibi_slopes(B_HEADS)
    ar = jnp.arange
    pairs = lambda make: jnp.stack([make(slopes[2 * hp:2 * hp + 2]) for hp in range(B_PAIRS)])
    b1 = pairs(lambda sl: _band_bias(sl, ar(B_TILE), ar(B_KEYS), (0, B_RADIUS, 2 * B_RADIUS), B_RADIUS, 1))
    a4 = ar(4)[:, None]
    q4 = (4 * ar(32)[None, :] + a4).reshape(-1)
    k4 = (4 * ar(64)[None, :] + a4).reshape(-1)
    b4 = pairs(lambda sl: _band_bias(sl, q4, k4, (0, 64, 128), B_RADIUS, 4))
    if seq // MAX_DIL == B_TILE:
        b16 = pairs(lambda sl: _band_bias(sl, ar(B_TILE), ar(B_TILE), (0,), B_RADIUS, 16))
    else:
        b16 = pairs(lambda sl: _band_bias(sl, ar(B_TILE), ar(B_KEYS), (0, B_RADIUS, 2 * B_RADIUS), B_RADIUS, 16))
    return b1, b4, b16


def _merge_ln_kernel(sink_ref, x_ref, qa_ref, ka_ref, va_ref, bias_a_ref, ob_ref, km_ref, vm_ref,
                     wqm_ref, wg_ref, wbr_ref, wo_ref, g_ref, b_ref, o_ref, *, seq):
    x = x_ref[0]
    xb = x.astype(BF16)
    low_half = lax.broadcasted_iota(jnp.int32, (1, LANES), 1) < HEAD_DIM
    blocks = []
    for j in range(MERGE_ROWS // A_BLK):
        start = pl.program_id(1) * MERGE_ROWS + j * A_BLK
        window = pl.multiple_of(jnp.clip(start - A_BLK, 0, seq - A_KEYS), A_BLK)
        blocks.append(_attn_a_block(
            qa_ref[0, j * A_BLK:(j + 1) * A_BLK, :], ka_ref[0, pl.ds(window, A_KEYS), :],
            va_ref[0, pl.ds(window, A_KEYS), :], bias_a_ref[_edge_variant(start, seq - A_BLK)],
            sink_ref, low_half))
    oa = jnp.concatenate(blocks, axis=0)
    qm = _dot(xb, wqm_ref[...]).astype(BF16)
    heads = []
    for h in range(M_HEADS):
        lanes = slice(h * M_HEAD_DIM, (h + 1) * M_HEAD_DIM)
        s = _dot_nt(qm[:, lanes], km_ref[0, :, lanes]) * (LOG2E * M_HEAD_DIM ** -0.5)
        p = jnp.exp2(s - jnp.max(s, axis=-1, keepdims=True))
        denom = jnp.sum(p, axis=-1, keepdims=True)
        heads.append(_dot(p.astype(BF16), vm_ref[0, :, lanes]) / denom)
    om = jnp.concatenate(heads, axis=-1).astype(BF16)
    ob = jnp.concatenate([ob_ref[0, hp] for hp in range(B_PAIRS)], axis=-1)
    mixed = None
    for i, branch in enumerate((oa, ob, om)):
        gate = jax.nn.sigmoid(_dot(xb, wg_ref[:, i * D_MODEL:(i + 1) * D_MODEL]))
        term = gate * _dot(branch, wbr_ref[i])
        mixed = term if mixed is None else mixed + term
    y = ALPHA * x + _dot(mixed.astype(BF16), wo_ref[...])
    o_ref[0] = _layer_norm(y, g_ref[...], b_ref[...])


def _merge_ln(sink, x, qa, ka, va, bias_a, ob, km, vm, w_qm, w_gate, w_branch, w_out, g, b):
    bsz, seq, _ = x.shape
    rows = MERGE_ROWS
    tok = lambda cols: pl.BlockSpec((1, rows, cols), lambda bi, i: (bi, i, 0))
    whole = pl.BlockSpec((1, seq, LANES), lambda bi, i: (bi, 0, 0))
    mem = pl.BlockSpec((1, N_MEM, QM_COLS), lambda bi, i: (bi, 0, 0))
    return pl.pallas_call(
        functools.partial(_merge_ln_kernel, seq=seq),
        grid=(bsz, seq // rows),
        in_specs=[pl.BlockSpec(memory_space=pltpu.SMEM), tok(D_MODEL), tok(QA_COLS), whole, whole,
                  _resident(bias_a.shape),
                  pl.BlockSpec((1, B_PAIRS, rows, LANES), lambda bi, i: (bi, 0, i, 0)),
                  mem, mem,
                  _resident(w_qm.shape), _resident(w_gate.shape), _resident(w_branch.shape),
                  _resident(w_out.shape), _resident((1, D_MODEL)), _resident((1, D_MODEL))],
        out_specs=tok(D_MODEL),
        out_shape=jax.ShapeDtypeStruct((bsz, seq, D_MODEL), F32),
        compiler_params=_params(2),
        name="merge_ln",
    )(sink, x, qa, ka, va, bias_a, ob, km, vm, w_qm, w_gate, w_branch, w_out, g, b)


def _slab_order():
    cols = []
    for i in range(A_GROUP):
        cols += list(range(i * HEAD_DIM, (i + 1) * HEAD_DIM))
        cols += list(range((A_GROUP + i) * HEAD_DIM, (A_GROUP + i + 1) * HEAD_DIM))
    return jnp.asarray(cols, dtype=jnp.int32)


def kernel(x_prompt, x_sample, mem_prompt, mem_sample, ffn1_w_in, ffn1_w_out, ln1_g, ln1_b, w_in, w_mem_kv,
           sink_a, w_branch, w_out, ln2_g, ln2_b, ffn2_w_in, ffn2_w_out, ln3_g, ln3_b):
    order = _slab_order()
    bias_a = _attn_a_bias()
    q_scale = LOG2E * HEAD_DIM ** -0.5

    layers = []
    for l in range(DEPTH):
        w = w_in[l]
        qa_w = w[:, :QA_COLS][:, order] * q_scale
        kva_w = w[:, QA_COLS:QA_COLS + KVA_COLS]
        qb_w = w[:, QA_COLS + KVA_COLS:QA_COLS + KVA_COLS + BRANCH_WIDTH] * q_scale
        kvb_w = w[:, QA_COLS + KVA_COLS + BRANCH_WIDTH:QM_START]
        w_proj = jnp.concatenate([qa_w, kva_w, qb_w, kvb_w], axis=1).astype(BF16)
        w_br = jnp.concatenate([w_branch[l, :1][:, order], w_branch[l, 1:]], axis=0).astype(BF16)
        row = lambda t: t[l].reshape(1, D_MODEL).astype(F32)
        layers.append(dict(
            ffn1=(ffn1_w_in[l].astype(BF16), (0.5 * ffn1_w_out[l]).astype(BF16), row(ln1_g), row(ln1_b)),
            ffn2=(ffn2_w_in[l].astype(BF16), (0.5 * ffn2_w_out[l]).astype(BF16), row(ln3_g), row(ln3_b)),
            w_proj=w_proj,
            w_mem=w_mem_kv[l].astype(BF16),
            sink=sink_a[l].astype(F32) * LOG2E,
            merge=(w[:, QM_START:GATE_START].astype(BF16), w[:, GATE_START:].astype(BF16), w_br,
                   w_out[l].astype(BF16), row(ln2_g), row(ln2_b)),
        ))

    def trunk(x, mem):
        bsz, seq, _ = x.shape
        biases_b = _attn_b_biases(seq)
        for p in layers:
            x, x_bf16 = _ffn_ln(x.reshape(bsz * seq, D_MODEL), *p["ffn1"], with_bf16_copy=True)
            x = x.reshape(bsz, seq, D_MODEL)
            qa, ka, va, *qkv_b = _in_proj(x_bf16.reshape(bsz, seq, D_MODEL), p["w_proj"])
            km, vm = _mem_kv(mem, p["w_mem"])
            ob = _attn_b(qkv_b[:3], qkv_b[3:], biases_b)
            x = _merge_ln(p["sink"], x, qa, ka, va, bias_a, ob, km, vm, *p["merge"])
            x = _ffn_ln(x.reshape(bsz * seq, D_MODEL), *p["ffn2"]).reshape(bsz, seq, D_MODEL)
        return x

    return trunk(x_prompt, mem_prompt), trunk(x_sample, mem_sample)
```

```python
import functools

import jax
import jax.numpy as jnp
from jax import lax
from jax.experimental import pallas as pl
from jax.experimental.pallas import tpu as pltpu

F32 = jnp.float32
BF16 = jnp.bfloat16

D_MODEL = 1024
DEPTH = 2
HEAD_DIM = 64
A_HEADS = 8
A_KV_HEADS = 2
A_GROUP = A_HEADS // A_KV_HEADS
A_RADIUS = 128
B_HEADS = 8
B_RADIUS = 64
M_HEADS = 4
M_HEAD_DIM = 128
N_MEM = 256
N_BRANCH = 3
BRANCH_WIDTH = 512
D_FF = 2816
LN_EPS = 1e-5
ALPHA = (2 * DEPTH) ** 0.25
NEG_INF = -1e30
LOG2E = 1.4426950408889634

QA_COLS = A_HEADS * HEAD_DIM
KVA_COLS = 2 * A_KV_HEADS * HEAD_DIM
QKVB_COLS = 3 * B_HEADS * HEAD_DIM
QM_COLS = M_HEADS * M_HEAD_DIM
GATE_COLS = N_BRANCH * D_MODEL
QM_START = QA_COLS + KVA_COLS + QKVB_COLS
GATE_START = QM_START + QM_COLS

LANES = 128
MAX_DIL = 16
B_PAIRS = B_HEADS * HEAD_DIM // LANES
A_SLABS = QA_COLS // LANES
PROJ_COLS = QA_COLS + KVA_COLS + QKVB_COLS

FFN_ROWS = 512
FFN_CHUNK = 256
LN_CHUNKS = 8
PROJ_ROWS = 1024
STAGE_PITCH = 72
MERGE_ROWS = 512
A_TILE = 1024
A_BLK = A_RADIUS
A_KEYS = 3 * A_BLK
B_TILE = 128
B_KEYS = 2 * B_TILE
SUPER = B_TILE * MAX_DIL
GROUP_PITCH = 24

VMEM_LIMIT = 56 * 1024 * 1024


def _resident(shape):
    zeros = (0,) * len(shape)
    return pl.BlockSpec(shape, lambda *_: zeros, pipeline_mode=pl.Buffered(1))


def _params(n_axes, semantics="parallel"):
    return pltpu.CompilerParams(dimension_semantics=(semantics,) * n_axes,
                                vmem_limit_bytes=VMEM_LIMIT)


def _layer_norm(y, g, b):
    mu = jnp.mean(y, axis=-1, keepdims=True)
    d = y - mu
    var = jnp.mean(d * d, axis=-1, keepdims=True)
    return d * lax.rsqrt(var + LN_EPS) * g + b


def _dot(a, b):
    return jnp.dot(a, b, preferred_element_type=F32)


def _dot_nt(a, b):
    return lax.dot_general(a, b, (((1,), (1,)), ((), ())), preferred_element_type=F32)


def _alibi_slopes(n):
    return 2.0 ** (-8.0 * jnp.arange(1, n + 1, dtype=F32) / n)


def _lagged_layer_norm(y_ref, g_ref, b_ref, store, n_chunks):
    rows = y_ref.shape[0] // n_chunks
    zeros = []
    for c in range(n_chunks):
        out = _layer_norm(y_ref[c * rows:(c + 1) * rows, :], g_ref[...], b_ref[...])
        store(c * rows, (c + 1) * rows, out)
        bits = pltpu.bitcast(out, jnp.int32)
        bits = functools.reduce(jnp.bitwise_or, [bits[:, t * LANES:(t + 1) * LANES] for t in range(D_MODEL // LANES)])
        bits = functools.reduce(jnp.bitwise_or, [bits[r * 8:(r + 1) * 8] for r in range(rows // 8)])
        zeros.append(lax.shift_right_logical(lax.shift_right_logical(bits, 16), 16).astype(F32))
    return zeros


def _anchored(val, zero_tile):
    top = jnp.concatenate([val[0:8, 0:LANES] + zero_tile, val[0:8, LANES:]], axis=1)
    return jnp.concatenate([top, val[8:]], axis=0)


def _ffn_ln_kernel(x_ref, win_ref, wout_ref, g_ref, b_ref, o_ref, y_ref, *, tiles):
    step = pl.program_id(0)

    def store(lo, hi, out):
        o_ref[lo:hi, :] = out

    @pl.when(step == 0)
    def _():
        y_ref[...] = jnp.zeros_like(y_ref)

    @pl.when(step < tiles)
    def _():
        zeros = _lagged_layer_norm(y_ref, g_ref, b_ref, store, LN_CHUNKS)
        x = x_ref[...]
        xb = x.astype(BF16)
        acc = ALPHA * x
        for c in range(D_FF // FFN_CHUNK):
            lo = c * FFN_CHUNK
            gate = _dot(xb, win_ref[:, lo:lo + FFN_CHUNK])
            up = _dot(xb, win_ref[:, D_FF + lo:D_FF + lo + FFN_CHUNK])
            if c < LN_CHUNKS:
                up = _anchored(up, zeros[c])
            act = (gate * jax.nn.sigmoid(gate) * up).astype(BF16)
            acc = acc + _dot(act, wout_ref[lo:lo + FFN_CHUNK, :])
        y_ref[...] = acc

    @pl.when(step == tiles)
    def _():
        _lagged_layer_norm(y_ref, g_ref, b_ref, store, LN_CHUNKS)


def _ffn_ln(x2d, w_in, w_out, g, b):
    rows = x2d.shape[0]
    tiles = rows // FFN_ROWS
    return pl.pallas_call(
        functools.partial(_ffn_ln_kernel, tiles=tiles),
        grid=(tiles + 1,),
        in_specs=[
            pl.BlockSpec((FFN_ROWS, D_MODEL), lambda s: (jnp.minimum(s, tiles - 1), 0)),
            _resident((D_MODEL, 2 * D_FF)),
            _resident((D_FF, D_MODEL)),
            _resident((1, D_MODEL)),
            _resident((1, D_MODEL)),
        ],
        out_specs=pl.BlockSpec((FFN_ROWS, D_MODEL), lambda s: (jnp.maximum(s - 1, 0), 0)),
        out_shape=jax.ShapeDtypeStruct((rows, D_MODEL), F32),
        scratch_shapes=[pltpu.VMEM((FFN_ROWS, D_MODEL), F32)],
        compiler_params=_params(1, "arbitrary"),
        name="ffn_ln",
    )(x2d, w_in, w_out, g, b)


def _in_proj_kernel(x_ref, w_ref, qa_ref, ka_ref, va_ref, qb_ref, kb_ref, vb_ref,
                    qr_ref, kr_ref, vr_ref, stage):
    xb = x_ref[0].astype(BF16)
    base = QA_COLS + KVA_COLS
    sub = PROJ_ROWS // MAX_DIL
    for n, (ref, res_ref) in enumerate(((qb_ref, qr_ref), (kb_ref, kr_ref), (vb_ref, vr_ref))):
        lo = base + n * BRANCH_WIDTH
        res = _dot(xb, w_ref[:, lo:lo + BRANCH_WIDTH])
        for hp in range(B_PAIRS):
            slab = res[:, hp * LANES:(hp + 1) * LANES]
            ref[0, hp] = slab.astype(BF16)
            for u in range(sub):
                stage[n, hp, pl.ds(u, MAX_DIL, stride=STAGE_PITCH), :] = slab[u * MAX_DIL:(u + 1) * MAX_DIL]
            for r in range(MAX_DIL):
                res_ref[0, hp, r] = stage[n, hp, r * STAGE_PITCH:r * STAGE_PITCH + sub, :].astype(BF16)
    kva = _dot(xb, w_ref[:, QA_COLS:base]).astype(BF16)
    ka_ref[0] = kva[:, :LANES]
    va_ref[0] = kva[:, LANES:]
    qa_ref[0] = _dot(xb, w_ref[:, 0:QA_COLS]).astype(BF16)


def _in_proj(x, w):
    bsz, seq, _ = x.shape
    rows = PROJ_ROWS
    sub = rows // MAX_DIL
    tok = lambda cols: pl.BlockSpec((1, rows, cols), lambda b, i: (b, i, 0))
    pair = pl.BlockSpec((1, B_PAIRS, rows, LANES), lambda b, i: (b, 0, i, 0))
    pair_shape = jax.ShapeDtypeStruct((bsz, B_PAIRS, seq, LANES), BF16)
    resid = pl.BlockSpec((1, B_PAIRS, MAX_DIL, sub, LANES), lambda b, i: (b, 0, 0, i, 0))
    resid_shape = jax.ShapeDtypeStruct((bsz, B_PAIRS, MAX_DIL, seq // MAX_DIL, LANES), BF16)
    return pl.pallas_call(
        _in_proj_kernel,
        grid=(bsz, seq // rows),
        in_specs=[tok(D_MODEL), _resident((D_MODEL, PROJ_COLS))],
        out_specs=[tok(QA_COLS), tok(LANES), tok(LANES), pair, pair, pair, resid, resid, resid],
        out_shape=[
            jax.ShapeDtypeStruct((bsz, seq, QA_COLS), BF16),
            jax.ShapeDtypeStruct((bsz, seq, LANES), BF16),
            jax.ShapeDtypeStruct((bsz, seq, LANES), BF16),
            pair_shape, pair_shape, pair_shape, resid_shape, resid_shape, resid_shape,
        ],
        scratch_shapes=[pltpu.VMEM((3, B_PAIRS, MAX_DIL * STAGE_PITCH, LANES), F32)],
        compiler_params=_params(2),
        name="in_proj",
    )(x, w)


def _mem_kv_kernel(mem_ref, w_ref, km_ref, vm_ref):
    res = _dot(mem_ref[0].astype(BF16), w_ref[...]).astype(BF16)
    km_ref[0] = res[:, :QM_COLS]
    vm_ref[0] = res[:, QM_COLS:]


def _mem_kv(mem, w):
    bsz = mem.shape[0]
    out = pl.BlockSpec((1, N_MEM, QM_COLS), lambda b: (b, 0, 0))
    shape = jax.ShapeDtypeStruct((bsz, N_MEM, QM_COLS), BF16)
    return pl.pallas_call(
        _mem_kv_kernel,
        grid=(bsz,),
        in_specs=[pl.BlockSpec((1, N_MEM, D_MODEL), lambda b: (b, 0, 0)),
                  _resident((D_MODEL, 2 * QM_COLS))],
        out_specs=[out, out],
        out_shape=[shape, shape],
        compiler_params=_params(1),
        name="mem_kv",
    )(mem, w)


def _edge_variant(start, last_start):
    return jnp.where(start == 0, 0, jnp.where(start == last_start, 2, 1))


def _band_bias(slopes, q_pos, k_pos, shifts, radius, dil):
    tables = []
    for shift in shifts:
        rel = k_pos[None, :] - shift - q_pos[:, None]
        dist = jnp.abs(rel).astype(F32) * dil
        table = jnp.where((jnp.abs(rel) <= radius)[None], -LOG2E * slopes[:, None, None] * dist[None], NEG_INF)
        tables.append(table.reshape(-1, k_pos.shape[0]))
    return jnp.stack(tables)


def _attn_a_block(q4, k3, v3, bias, sink_ref, low_half):
    zero = jnp.zeros((), BF16)
    lhs = jnp.concatenate(
        [jnp.where(low_half if g == 0 else jnp.logical_not(low_half), q4[:, i * LANES:(i + 1) * LANES], zero)
         for g in range(A_KV_HEADS) for i in range(A_SLABS)], axis=0)
    s = _dot_nt(lhs, k3) + bias
    probs, totals = [], []
    for h in range(A_HEADS):
        sh = s[h * A_BLK:(h + 1) * A_BLK]
        sink = sink_ref[h]
        m = jnp.maximum(jnp.max(sh, axis=-1, keepdims=True), sink)
        p = jnp.exp2(sh - m)
        totals.append(jnp.sum(p, axis=-1, keepdims=True) + jnp.exp2(sink - m))
        probs.append(p.astype(BF16))
    o = _dot(jnp.concatenate(probs, axis=0), v3)
    slabs = []
    for i in range(A_SLABS):
        lo, hi = i, A_GROUP + i
        num = jnp.where(low_half, o[lo * A_BLK:(lo + 1) * A_BLK], o[hi * A_BLK:(hi + 1) * A_BLK])
        den = jnp.where(low_half, totals[lo], totals[hi])
        slabs.append((num / den).astype(BF16))
    return jnp.concatenate(slabs, axis=-1)


def _attn_a_bias():
    ar = jnp.arange
    return _band_bias(_alibi_slopes(A_HEADS), ar(A_BLK), ar(A_KEYS), (0, A_BLK, 2 * A_BLK), A_RADIUS, 1)


def _pair_attend(q2, k2, v2, bias, low_half):
    n = q2.shape[0]
    zero = jnp.zeros((), BF16)
    lhs = jnp.concatenate([jnp.where(low_half, q2, zero), jnp.where(low_half, zero, q2)], axis=0)
    s = _dot_nt(lhs, k2) + bias
    m = jnp.max(s, axis=-1, keepdims=True)
    p = jnp.exp2(s - m)
    denom = jnp.sum(p, axis=-1, keepdims=True)
    o = _dot(p.astype(BF16), v2)
    denom = jnp.where(low_half, denom[:n], denom[n:])
    out = jnp.where(low_half, o[:n], o[n:]) / denom
    lse = jnp.where(low_half, m[:n], m[n:]) + jnp.log(denom) * LOG2E
    return out, lse


def _attn_b_kernel(q_ref, k_ref, v_ref, qr_ref, kr_ref, vr_ref, b1_ref, b4_ref, b16_ref, o_ref, osc, lsc,
                   *, seq):
    rows = seq // MAX_DIL
    qs, ks, vs = qr_ref.at[0, 0], kr_ref.at[0, 0], vr_ref.at[0, 0]
    low_half = lax.broadcasted_iota(jnp.int32, (1, LANES), 1) < HEAD_DIM
    groups_per_tile = B_TILE // MAX_DIL

    def super_tile(st, carry):
        t0 = pl.multiple_of(st * SUPER, SUPER)
        u0 = pl.multiple_of(st * B_TILE, B_TILE)

        whole = rows == B_TILE
        keys16 = B_TILE if whole else B_KEYS
        win16 = 0 if whole else pl.multiple_of(jnp.clip(u0 - B_RADIUS, 0, rows - B_KEYS), B_RADIUS)

        def d16(r, c):
            q2 = qs[r, pl.ds(u0, B_TILE), :]
            k2 = ks[r, pl.ds(win16, keys16), :]
            v2 = vs[r, pl.ds(win16, keys16), :]
            bias = b16_ref[0, 0 if whole else _edge_variant(u0, rows - B_TILE)]
            out, lse = _pair_attend(q2, k2, v2, bias, low_half)
            osc[2, pl.ds(r, B_TILE, stride=GROUP_PITCH), :] = out
            lsc[2, pl.ds(r, B_TILE, stride=GROUP_PITCH), :] = lse
            return c

        lax.fori_loop(0, MAX_DIL, d16, 0, unroll=16)

        def d4(n, c):
            r4 = n // 4
            part = n % 4
            u1 = pl.multiple_of(u0 + part * 32, 32)
            win = pl.multiple_of(jnp.clip(u1 - 16, 0, rows - 64), 16)
            q2 = jnp.concatenate([qs[4 * a + r4, pl.ds(u1, 32), :] for a in range(4)], axis=0)
            k2 = jnp.concatenate([ks[4 * a + r4, pl.ds(win, 64), :] for a in range(4)], axis=0)
            v2 = jnp.concatenate([vs[4 * a + r4, pl.ds(win, 64), :] for a in range(4)], axis=0)
            out, lse = _pair_attend(q2, k2, v2, b4_ref[0, _edge_variant(u1, rows - 32)], low_half)
            for a in range(4):
                dst = pl.ds(part * (32 * GROUP_PITCH) + 4 * a + r4, 32, stride=GROUP_PITCH)
                osc[1, dst, :] = out[a * 32:(a + 1) * 32]
                lsc[1, dst, :] = lse[a * 32:(a + 1) * 32]
            return c

        lax.fori_loop(0, 16, d4, 0, unroll=16)

        def d1(n, c):
            off = pl.multiple_of(n * B_TILE, B_TILE)
            start = t0 + off
            win = pl.multiple_of(jnp.clip(start - B_RADIUS, 0, seq - B_KEYS), B_RADIUS)
            q2 = q_ref[0, 0, pl.ds(start, B_TILE), :]
            k2 = k_ref[0, 0, pl.ds(win, B_KEYS), :]
            v2 = v_ref[0, 0, pl.ds(win, B_KEYS), :]
            out, lse = _pair_attend(q2, k2, v2, b1_ref[0, _edge_variant(start, seq - B_TILE)], low_half)
            base = pl.multiple_of(n * (groups_per_tile * GROUP_PITCH), 8)
            for g in range(groups_per_tile):
                dst = pl.ds(base + g * GROUP_PITCH, MAX_DIL)
                osc[0, dst, :] = out[g * MAX_DIL:(g + 1) * MAX_DIL]
                lsc[0, dst, :] = lse[g * MAX_DIL:(g + 1) * MAX_DIL]
            return c

        lax.fori_loop(0, SUPER // B_TILE, d1, 0, unroll=16)

        def groups(ref, c):
            return jnp.concatenate(
                [ref[c, g * GROUP_PITCH:g * GROUP_PITCH + MAX_DIL, :] for g in range(SUPER // MAX_DIL)], axis=0)

        l0, l1, l2 = groups(lsc, 0), groups(lsc, 1), groups(lsc, 2)
        m = jnp.maximum(jnp.maximum(l0, l1), l2)
        e0, e1, e2 = jnp.exp2(l0 - m), jnp.exp2(l1 - m), jnp.exp2(l2 - m)
        mixed = (e0 * groups(osc, 0) + e1 * groups(osc, 1) + e2 * groups(osc, 2)) / (e0 + e1 + e2)
        o_ref[0, 0, pl.ds(t0, SUPER), :] = mixed.astype(BF16)
        return carry

    lax.fori_loop(0, seq // SUPER, super_tile, 0)


def _attn_b(qkv, qkv_resid, biases):
    bsz, _, seq, _ = qkv[0].shape
    rows = seq // MAX_DIL
    spec = pl.BlockSpec((1, 1, seq, LANES), lambda hp, b: (b, hp, 0, 0))
    rspec = pl.BlockSpec((1, 1, MAX_DIL, rows, LANES), lambda hp, b: (b, hp, 0, 0, 0))
    bias_specs = [pl.BlockSpec((1,) + t.shape[1:], lambda hp, b: (hp, 0, 0, 0)) for t in biases]
    return pl.pallas_call(
        functools.partial(_attn_b_kernel, seq=seq),
        grid=(B_PAIRS, bsz),
        in_specs=[spec] * 3 + [rspec] * 3 + bias_specs,
        out_specs=spec,
        out_shape=jax.ShapeDtypeStruct((bsz, B_PAIRS, seq, LANES), BF16),
        scratch_shapes=[
            pltpu.VMEM((3, SUPER // MAX_DIL * GROUP_PITCH, LANES), F32),
            pltpu.VMEM((3, SUPER // MAX_DIL * GROUP_PITCH, LANES), F32),
        ],
        compiler_params=_params(2),
        name="attn_b",
    )(*qkv, *qkv_resid, *biases)


def _attn_b_biases(seq):
    slopes = _alibi_slopes(B_HEADS)
    ar = jnp.arange
    pairs = lambda make: jnp.stack([make(slopes[2 * hp:2 * hp + 2]) for hp in range(B_PAIRS)])
    b1 = pairs(lambda sl: _band_bias(sl, ar(B_TILE), ar(B_KEYS), (0, B_RADIUS, 2 * B_RADIUS), B_RADIUS, 1))
    a4 = ar(4)[:, None]
    q4 = (4 * ar(32)[None, :] + a4).reshape(-1)
    k4 = (4 * ar(64)[None, :] + a4).reshape(-1)
    b4 = pairs(lambda sl: _band_bias(sl, q4, k4, (0, 64, 128), B_RADIUS, 4))
    if seq // MAX_DIL == B_TILE:
        b16 = pairs(lambda sl: _band_bias(sl, ar(B_TILE), ar(B_TILE), (0,), B_RADIUS, 16))
    else:
        b16 = pairs(lambda sl: _band_bias(sl, ar(B_TILE), ar(B_KEYS), (0, B_RADIUS, 2 * B_RADIUS), B_RADIUS, 16))
    return b1, b4, b16


def _merge_ln_kernel(sink_ref, x_ref, qa_ref, ka_ref, va_ref, bias_a_ref, ob_ref, km_ref, vm_ref,
                     wqm_ref, wg_ref, wbr_ref, wo_ref, g_ref, b_ref, o_ref, *, seq):
    x = x_ref[0]
    xb = x.astype(BF16)
    low_half = lax.broadcasted_iota(jnp.int32, (1, LANES), 1) < HEAD_DIM
    blocks = []
    for j in range(MERGE_ROWS // A_BLK):
        start = pl.program_id(1) * MERGE_ROWS + j * A_BLK
        window = pl.multiple_of(jnp.clip(start - A_BLK, 0, seq - A_KEYS), A_BLK)
        blocks.append(_attn_a_block(
            qa_ref[0, j * A_BLK:(j + 1) * A_BLK, :], ka_ref[0, pl.ds(window, A_KEYS), :],
            va_ref[0, pl.ds(window, A_KEYS), :], bias_a_ref[_edge_variant(start, seq - A_BLK)],
            sink_ref, low_half))
    oa = jnp.concatenate(blocks, axis=0)
    qm = _dot(xb, wqm_ref[...]).astype(BF16)
    heads = []
    for h in range(M_HEADS):
        lanes = slice(h * M_HEAD_DIM, (h + 1) * M_HEAD_DIM)
        s = _dot_nt(qm[:, lanes], km_ref[0, :, lanes]) * (LOG2E * M_HEAD_DIM ** -0.5)
        p = jnp.exp2(s - jnp.max(s, axis=-1, keepdims=True))
        denom = jnp.sum(p, axis=-1, keepdims=True)
        heads.append(_dot(p.astype(BF16), vm_ref[0, :, lanes]) / denom)
    om = jnp.concatenate(heads, axis=-1).astype(BF16)
    ob = jnp.concatenate([ob_ref[0, hp] for hp in range(B_PAIRS)], axis=-1)
    mixed = None
    for i, branch in enumerate((oa, ob, om)):
        gate = jax.nn.sigmoid(_dot(xb, wg_ref[:, i * D_MODEL:(i + 1) * D_MODEL]))
        term = gate * _dot(branch, wbr_ref[i])
        mixed = term if mixed is None else mixed + term
    y = ALPHA * x + _dot(mixed.astype(BF16), wo_ref[...])
    o_ref[0] = _layer_norm(y, g_ref[...], b_ref[...])


def _merge_ln(sink, x, qa, ka, va, bias_a, ob, km, vm, w_qm, w_gate, w_branch, w_out, g, b):
    bsz, seq, _ = x.shape
    rows = MERGE_ROWS
    tok = lambda cols: pl.BlockSpec((1, rows, cols), lambda bi, i: (bi, i, 0))
    whole = pl.BlockSpec((1, seq, LANES), lambda bi, i: (bi, 0, 0))
    mem = pl.BlockSpec((1, N_MEM, QM_COLS), lambda bi, i: (bi, 0, 0))
    return pl.pallas_call(
        functools.partial(_merge_ln_kernel, seq=seq),
        grid=(bsz, seq // rows),
        in_specs=[pl.BlockSpec(memory_space=pltpu.SMEM), tok(D_MODEL), tok(QA_COLS), whole, whole,
                  _resident(bias_a.shape),
                  pl.BlockSpec((1, B_PAIRS, rows, LANES), lambda bi, i: (bi, 0, i, 0)),
                  mem, mem,
                  _resident(w_qm.shape), _resident(w_gate.shape), _resident(w_branch.shape),
                  _resident(w_out.shape), _resident((1, D_MODEL)), _resident((1, D_MODEL))],
        out_specs=tok(D_MODEL),
        out_shape=jax.ShapeDtypeStruct((bsz, seq, D_MODEL), F32),
        compiler_params=_params(2),
        name="merge_ln",
    )(sink, x, qa, ka, va, bias_a, ob, km, vm, w_qm, w_gate, w_branch, w_out, g, b)


def _slab_order():
    cols = []
    for i in range(A_GROUP):
        cols += list(range(i * HEAD_DIM, (i + 1) * HEAD_DIM))
        cols += list(range((A_GROUP + i) * HEAD_DIM, (A_GROUP + i + 1) * HEAD_DIM))
    return jnp.asarray(cols, dtype=jnp.int32)


def kernel(x_prompt, x_sample, mem_prompt, mem_sample, ffn1_w_in, ffn1_w_out, ln1_g, ln1_b, w_in, w_mem_kv,
           sink_a, w_branch, w_out, ln2_g, ln2_b, ffn2_w_in, ffn2_w_out, ln3_g, ln3_b):
    order = _slab_order()
    bias_a = _attn_a_bias()
    q_scale = LOG2E * HEAD_DIM ** -0.5

    layers = []
    for l in range(DEPTH):
        w = w_in[l]
        qa_w = w[:, :QA_COLS][:, order] * q_scale
        kva_w = w[:, QA_COLS:QA_COLS + KVA_COLS]
        qb_w = w[:, QA_COLS + KVA_COLS:QA_COLS + KVA_COLS + BRANCH_WIDTH] * q_scale
        kvb_w = w[:, QA_COLS + KVA_COLS + BRANCH_WIDTH:QM_START]
        w_proj = jnp.concatenate([qa_w, kva_w, qb_w, kvb_w], axis=1).astype(BF16)
        w_br = jnp.concatenate([w_branch[l, :1][:, order], w_branch[l, 1:]], axis=0).astype(BF16)
        row = lambda t: t[l].reshape(1, D_MODEL).astype(F32)
        layers.append(dict(
            ffn1=(ffn1_w_in[l].astype(BF16), (0.5 * ffn1_w_out[l]).astype(BF16), row(ln1_g), row(ln1_b)),
            ffn2=(ffn2_w_in[l].astype(BF16), (0.5 * ffn2_w_out[l]).astype(BF16), row(ln3_g), row(ln3_b)),
            w_proj=w_proj,
            w_mem=w_mem_kv[l].astype(BF16),
            sink=sink_a[l].astype(F32) * LOG2E,
            merge=(w[:, QM_START:GATE_START].astype(BF16), w[:, GATE_START:].astype(BF16), w_br,
                   w_out[l].astype(BF16), row(ln2_g), row(ln2_b)),
        ))

    def trunk(x, mem):
        bsz, seq, _ = x.shape
        biases_b = _attn_b_biases(seq)
        for p in layers:
            x = _ffn_ln(x.reshape(bsz * seq, D_MODEL), *p["ffn1"]).reshape(bsz, seq, D_MODEL)
            qa, ka, va, *qkv_b = _in_proj(x, p["w_proj"])
            km, vm = _mem_kv(mem, p["w_mem"])
            ob = _attn_b(qkv_b[:3], qkv_b[3:], biases_b)
            x = _merge_ln(p["sink"], x, qa, ka, va, bias_a, ob, km, vm, *p["merge"])
            x = _ffn_ln(x.reshape(bsz * seq, D_MODEL), *p["ffn2"]).reshape(bsz, seq, D_MODEL)
        return x

    return trunk(x_prompt, mem_prompt), trunk(x_sample, mem_sample)
```

```python
import functools

import jax
import jax.numpy as jnp
from jax import lax
from jax.experimental import pallas as pl
from jax.experimental.pallas import tpu as pltpu

F32 = jnp.float32
BF16 = jnp.bfloat16

D_MODEL = 1024
DEPTH = 2
HEAD_DIM = 64
A_HEADS = 8
A_KV_HEADS = 2
A_GROUP = A_HEADS // A_KV_HEADS
A_RADIUS = 128
B_HEADS = 8
B_RADIUS = 64
M_HEADS = 4
M_HEAD_DIM = 128
N_MEM = 256
N_BRANCH = 3
BRANCH_WIDTH = 512
D_FF = 2816
LN_EPS = 1e-5
ALPHA = (2 * DEPTH) ** 0.25
NEG_INF = -1e30
LOG2E = 1.4426950408889634

QA_COLS = A_HEADS * HEAD_DIM
KVA_COLS = 2 * A_KV_HEADS * HEAD_DIM
QKVB_COLS = 3 * B_HEADS * HEAD_DIM
QM_COLS = M_HEADS * M_HEAD_DIM
GATE_COLS = N_BRANCH * D_MODEL
QM_START = QA_COLS + KVA_COLS + QKVB_COLS
GATE_START = QM_START + QM_COLS

LANES = 128
MAX_DIL = 16
B_PAIRS = B_HEADS * HEAD_DIM // LANES
A_SLABS = QA_COLS // LANES
PROJ_COLS = QA_COLS + KVA_COLS + QKVB_COLS

FFN_ROWS = 512
FFN_CHUNK = 256
LN_CHUNKS = 8
PROJ_ROWS = 512
STAGE_PITCH = 40
MERGE_ROWS = 512
A_TILE = 1024
A_BLK = A_RADIUS
A_KEYS = 3 * A_BLK
B_TILE = 128
B_KEYS = 2 * B_TILE
SUPER = B_TILE * MAX_DIL
GROUP_PITCH = 24

VMEM_LIMIT = 56 * 1024 * 1024


def _resident(shape):
    zeros = (0,) * len(shape)
    return pl.BlockSpec(shape, lambda *_: zeros, pipeline_mode=pl.Buffered(1))


def _params(n_axes, semantics="parallel"):
    return pltpu.CompilerParams(dimension_semantics=(semantics,) * n_axes,
                                vmem_limit_bytes=VMEM_LIMIT)


def _layer_norm(y, g, b):
    mu = jnp.mean(y, axis=-1, keepdims=True)
    d = y - mu
    var = jnp.mean(d * d, axis=-1, keepdims=True)
    return d * lax.rsqrt(var + LN_EPS) * g + b


def _dot(a, b):
    return jnp.dot(a, b, preferred_element_type=F32)


def _dot_nt(a, b):
    return lax.dot_general(a, b, (((1,), (1,)), ((), ())), preferred_element_type=F32)


def _alibi_slopes(n):
    return 2.0 ** (-8.0 * jnp.arange(1, n + 1, dtype=F32) / n)


def _lagged_layer_norm(y_ref, g_ref, b_ref, store, n_chunks):
    rows = y_ref.shape[0] // n_chunks
    zeros = []
    for c in range(n_chunks):
        out = _layer_norm(y_ref[c * rows:(c + 1) * rows, :], g_ref[...], b_ref[...])
        store(c * rows, (c + 1) * rows, out)
        bits = pltpu.bitcast(out, jnp.int32)
        bits = functools.reduce(jnp.bitwise_or, [bits[:, t * LANES:(t + 1) * LANES] for t in range(D_MODEL // LANES)])
        bits = functools.reduce(jnp.bitwise_or, [bits[r * 8:(r + 1) * 8] for r in range(rows // 8)])
        zeros.append(lax.shift_right_logical(lax.shift_right_logical(bits, 16), 16).astype(F32))
    return zeros


def _anchored(val, zero_tile):
    top = jnp.concatenate([val[0:8, 0:LANES] + zero_tile, val[0:8, LANES:]], axis=1)
    return jnp.concatenate([top, val[8:]], axis=0)


def _ffn_tile(x, win_ref, wout_ref, zeros):
    xb = x.astype(BF16)
    acc = ALPHA * x
    for c in range(D_FF // FFN_CHUNK):
        lo = c * FFN_CHUNK
        gate = _dot(xb, win_ref[:, lo:lo + FFN_CHUNK])
        up = _dot(xb, win_ref[:, D_FF + lo:D_FF + lo + FFN_CHUNK])
        if c < LN_CHUNKS:
            up = _anchored(up, zeros[c])
        act = (gate * jax.nn.sigmoid(gate) * up).astype(BF16)
        acc = acc + _dot(act, wout_ref[lo:lo + FFN_CHUNK, :])
    return acc


def _ffn_ln_kernel(x_ref, win_ref, wout_ref, g_ref, b_ref, o_ref, y_ref, *, tiles):
    step = pl.program_id(0)

    def store(lo, hi, out):
        o_ref[lo:hi, :] = out

    @pl.when(step == 0)
    def _():
        y_ref[...] = jnp.zeros_like(y_ref)

    @pl.when(step < tiles)
    def _():
        zeros = _lagged_layer_norm(y_ref, g_ref, b_ref, store, LN_CHUNKS)
        y_ref[...] = _ffn_tile(x_ref[...], win_ref, wout_ref, zeros)

    @pl.when(step == tiles)
    def _():
        _lagged_layer_norm(y_ref, g_ref, b_ref, store, LN_CHUNKS)


def _ffn_ln(x2d, w_in, w_out, g, b):
    rows = x2d.shape[0]
    tiles = rows // FFN_ROWS
    return pl.pallas_call(
        functools.partial(_ffn_ln_kernel, tiles=tiles),
        grid=(tiles + 1,),
        in_specs=[
            pl.BlockSpec((FFN_ROWS, D_MODEL), lambda s: (jnp.minimum(s, tiles - 1), 0)),
            _resident((D_MODEL, 2 * D_FF)),
            _resident((D_FF, D_MODEL)),
            _resident((1, D_MODEL)),
            _resident((1, D_MODEL)),
        ],
        out_specs=pl.BlockSpec((FFN_ROWS, D_MODEL), lambda s: (jnp.maximum(s - 1, 0), 0)),
        out_shape=jax.ShapeDtypeStruct((rows, D_MODEL), F32),
        scratch_shapes=[pltpu.VMEM((FFN_ROWS, D_MODEL), F32)],
        compiler_params=_params(1, "arbitrary"),
        name="ffn_ln",
    )(x2d, w_in, w_out, g, b)


def _project(xb, w_ref, qa_ref, ka_ref, va_ref, qb_ref, kb_ref, vb_ref, qr_ref, kr_ref, vr_ref, stage):
    base = QA_COLS + KVA_COLS
    sub = PROJ_ROWS // MAX_DIL
    for n, (ref, res_ref) in enumerate(((qb_ref, qr_ref), (kb_ref, kr_ref), (vb_ref, vr_ref))):
        lo = base + n * BRANCH_WIDTH
        res = _dot(xb, w_ref[:, lo:lo + BRANCH_WIDTH])
        for hp in range(B_PAIRS):
            slab = res[:, hp * LANES:(hp + 1) * LANES]
            ref[0, hp] = slab.astype(BF16)
            for u in range(sub):
                stage[n, hp, pl.ds(u, MAX_DIL, stride=STAGE_PITCH), :] = slab[u * MAX_DIL:(u + 1) * MAX_DIL]
            for r in range(MAX_DIL):
                res_ref[0, hp, r] = stage[n, hp, r * STAGE_PITCH:r * STAGE_PITCH + sub, :].astype(BF16)
    kva = _dot(xb, w_ref[:, QA_COLS:base]).astype(BF16)
    ka_ref[0] = kva[:, :LANES]
    va_ref[0] = kva[:, LANES:]
    qa_ref[0] = _dot(xb, w_ref[:, 0:QA_COLS]).astype(BF16)


def _ffn_ln_proj_kernel(x_ref, win_ref, wout_ref, g_ref, b_ref, wproj_ref, o_ref, *rest, tiles):
    proj_refs, y_ref, stage = rest[:9], rest[9], rest[10]
    step = pl.program_id(0)

    def store(lo, hi, out):
        o_ref[0, lo:hi, :] = out

    def norm_and_project():
        zeros = _lagged_layer_norm(y_ref, g_ref, b_ref, store, LN_CHUNKS)
        _project(o_ref[0].astype(BF16), wproj_ref, *proj_refs, stage)
        return zeros

    @pl.when(step == 0)
    def _():
        y_ref[...] = jnp.zeros_like(y_ref)

    @pl.when(step < tiles)
    def _():
        y_ref[...] = _ffn_tile(x_ref[0], win_ref, wout_ref, norm_and_project())

    @pl.when(step == tiles)
    def _():
        norm_and_project()


def _ffn_ln_proj(x, w_in, w_out, g, b, w_proj):
    bsz, seq, _ = x.shape
    rows = PROJ_ROWS
    per_seq = seq // rows
    tiles = bsz * per_seq
    sub = rows // MAX_DIL
    cur = lambda s: jnp.minimum(s, tiles - 1)
    lag = lambda s: jnp.maximum(s - 1, 0)
    tok = lambda cols: pl.BlockSpec((1, rows, cols), lambda s: (lag(s) // per_seq, lag(s) % per_seq, 0))
    pair = pl.BlockSpec((1, B_PAIRS, rows, LANES), lambda s: (lag(s) // per_seq, 0, lag(s) % per_seq, 0))
    pair_shape = jax.ShapeDtypeStruct((bsz, B_PAIRS, seq, LANES), BF16)
    resid = pl.BlockSpec((1, B_PAIRS, MAX_DIL, sub, LANES),
                         lambda s: (lag(s) // per_seq, 0, 0, lag(s) % per_seq, 0))
    resid_shape = jax.ShapeDtypeStruct((bsz, B_PAIRS, MAX_DIL, seq // MAX_DIL, LANES), BF16)
    return pl.pallas_call(
        functools.partial(_ffn_ln_proj_kernel, tiles=tiles),
        grid=(tiles + 1,),
        in_specs=[pl.BlockSpec((1, rows, D_MODEL), lambda s: (cur(s) // per_seq, cur(s) % per_seq, 0)),
                  _resident((D_MODEL, 2 * D_FF)), _resident((D_FF, D_MODEL)),
                  _resident((1, D_MODEL)), _resident((1, D_MODEL)), _resident((D_MODEL, PROJ_COLS))],
        out_specs=[tok(D_MODEL), tok(QA_COLS), tok(LANES), tok(LANES), pair, pair, pair, resid, resid, resid],
        out_shape=[
            jax.ShapeDtypeStruct((bsz, seq, D_MODEL), F32),
            jax.ShapeDtypeStruct((bsz, seq, QA_COLS), BF16),
            jax.ShapeDtypeStruct((bsz, seq, LANES), BF16),
            jax.ShapeDtypeStruct((bsz, seq, LANES), BF16),
            pair_shape, pair_shape, pair_shape, resid_shape, resid_shape, resid_shape,
        ],
        scratch_shapes=[pltpu.VMEM((rows, D_MODEL), F32),
                        pltpu.VMEM((3, B_PAIRS, MAX_DIL * STAGE_PITCH, LANES), F32)],
        compiler_params=_params(1, "arbitrary"),
        name="ffn_ln_proj",
    )(x, w_in, w_out, g, b, w_proj)


def _mem_kv_kernel(mem_ref, w_ref, km_ref, vm_ref):
    res = _dot(mem_ref[0].astype(BF16), w_ref[...]).astype(BF16)
    km_ref[0] = res[:, :QM_COLS]
    vm_ref[0] = res[:, QM_COLS:]


def _mem_kv(mem, w):
    bsz = mem.shape[0]
    out = pl.BlockSpec((1, N_MEM, QM_COLS), lambda b: (b, 0, 0))
    shape = jax.ShapeDtypeStruct((bsz, N_MEM, QM_COLS), BF16)
    return pl.pallas_call(
        _mem_kv_kernel,
        grid=(bsz,),
        in_specs=[pl.BlockSpec((1, N_MEM, D_MODEL), lambda b: (b, 0, 0)),
                  _resident((D_MODEL, 2 * QM_COLS))],
        out_specs=[out, out],
        out_shape=[shape, shape],
        compiler_params=_params(1),
        name="mem_kv",
    )(mem, w)


def _edge_variant(start, last_start):
    return jnp.where(start == 0, 0, jnp.where(start == last_start, 2, 1))


def _band_bias(slopes, q_pos, k_pos, shifts, radius, dil):
    tables = []
    for shift in shifts:
        rel = k_pos[None, :] - shift - q_pos[:, None]
        dist = jnp.abs(rel).astype(F32) * dil
        table = jnp.where((jnp.abs(rel) <= radius)[None], -LOG2E * slopes[:, None, None] * dist[None], NEG_INF)
        tables.append(table.reshape(-1, k_pos.shape[0]))
    return jnp.stack(tables)


def _attn_a_block(q4, k3, v3, bias, sink_ref, low_half):
    zero = jnp.zeros((), BF16)
    lhs = jnp.concatenate(
        [jnp.where(low_half if g == 0 else jnp.logical_not(low_half), q4[:, i * LANES:(i + 1) * LANES], zero)
         for g in range(A_KV_HEADS) for i in range(A_SLABS)], axis=0)
    s = _dot_nt(lhs, k3) + bias
    probs, totals = [], []
    for h in range(A_HEADS):
        sh = s[h * A_BLK:(h + 1) * A_BLK]
        sink = sink_ref[h]
        m = jnp.maximum(jnp.max(sh, axis=-1, keepdims=True), sink)
        p = jnp.exp2(sh - m)
        totals.append(jnp.sum(p, axis=-1, keepdims=True) + jnp.exp2(sink - m))
        probs.append(p.astype(BF16))
    o = _dot(jnp.concatenate(probs, axis=0), v3)
    slabs = []
    for i in range(A_SLABS):
        lo, hi = i, A_GROUP + i
        num = jnp.where(low_half, o[lo * A_BLK:(lo + 1) * A_BLK], o[hi * A_BLK:(hi + 1) * A_BLK])
        den = jnp.where(low_half, totals[lo], totals[hi])
        slabs.append((num / den).astype(BF16))
    return jnp.concatenate(slabs, axis=-1)


def _attn_a_bias():
    ar = jnp.arange
    return _band_bias(_alibi_slopes(A_HEADS), ar(A_BLK), ar(A_KEYS), (0, A_BLK, 2 * A_BLK), A_RADIUS, 1)


def _pair_attend(q2, k2, v2, bias, low_half):
    n = q2.shape[0]
    zero = jnp.zeros((), BF16)
    lhs = jnp.concatenate([jnp.where(low_half, q2, zero), jnp.where(low_half, zero, q2)], axis=0)
    s = _dot_nt(lhs, k2) + bias
    m = jnp.max(s, axis=-1, keepdims=True)
    p = jnp.exp2(s - m)
    denom = jnp.sum(p, axis=-1, keepdims=True)
    o = _dot(p.astype(BF16), v2)
    denom = jnp.where(low_half, denom[:n], denom[n:])
    out = jnp.where(low_half, o[:n], o[n:]) / denom
    lse = jnp.where(low_half, m[:n], m[n:]) + jnp.log(denom) * LOG2E
    return out, lse


def _attn_b_kernel(q_ref, k_ref, v_ref, qr_ref, kr_ref, vr_ref, b1_ref, b4_ref, b16_ref, o_ref, osc, lsc,
                   *, seq):
    rows = seq // MAX_DIL
    qs, ks, vs = qr_ref.at[0, 0], kr_ref.at[0, 0], vr_ref.at[0, 0]
    low_half = lax.broadcasted_iota(jnp.int32, (1, LANES), 1) < HEAD_DIM
    groups_per_tile = B_TILE // MAX_DIL

    def super_tile(st, carry):
        t0 = pl.multiple_of(st * SUPER, SUPER)
        u0 = pl.multiple_of(st * B_TILE, B_TILE)

        whole = rows == B_TILE
        keys16 = B_TILE if whole else B_KEYS
        win16 = 0 if whole else pl.multiple_of(jnp.clip(u0 - B_RADIUS, 0, rows - B_KEYS), B_RADIUS)

        def d16(r, c):
            q2 = qs[r, pl.ds(u0, B_TILE), :]
            k2 = ks[r, pl.ds(win16, keys16), :]
            v2 = vs[r, pl.ds(win16, keys16), :]
            bias = b16_ref[0, 0 if whole else _edge_variant(u0, rows - B_TILE)]
            out, lse = _pair_attend(q2, k2, v2, bias, low_half)
            osc[2, pl.ds(r, B_TILE, stride=GROUP_PITCH), :] = out
            lsc[2, pl.ds(r, B_TILE, stride=GROUP_PITCH), :] = lse
            return c

        lax.fori_loop(0, MAX_DIL, d16, 0, unroll=16)

        def d4(n, c):
            r4 = n // 4
            part = n % 4
            u1 = pl.multiple_of(u0 + part * 32, 32)
            win = pl.multiple_of(jnp.clip(u1 - 16, 0, rows - 64), 16)
            q2 = jnp.concatenate([qs[4 * a + r4, pl.ds(u1, 32), :] for a in range(4)], axis=0)
            k2 = jnp.concatenate([ks[4 * a + r4, pl.ds(win, 64), :] for a in range(4)], axis=0)
            v2 = jnp.concatenate([vs[4 * a + r4, pl.ds(win, 64), :] for a in range(4)], axis=0)
            out, lse = _pair_attend(q2, k2, v2, b4_ref[0, _edge_variant(u1, rows - 32)], low_half)
            for a in range(4):
                dst = pl.ds(part * (32 * GROUP_PITCH) + 4 * a + r4, 32, stride=GROUP_PITCH)
                osc[1, dst, :] = out[a * 32:(a + 1) * 32]
                lsc[1, dst, :] = lse[a * 32:(a + 1) * 32]
            return c

        lax.fori_loop(0, 16, d4, 0, unroll=16)

        def d1(n, c):
            off = pl.multiple_of(n * B_TILE, B_TILE)
            start = t0 + off
            win = pl.multiple_of(jnp.clip(start - B_RADIUS, 0, seq - B_KEYS), B_RADIUS)
            q2 = q_ref[0, 0, pl.ds(start, B_TILE), :]
            k2 = k_ref[0, 0, pl.ds(win, B_KEYS), :]
            v2 = v_ref[0, 0, pl.ds(win, B_KEYS), :]
            out, lse = _pair_attend(q2, k2, v2, b1_ref[0, _edge_variant(start, seq - B_TILE)], low_half)
            base = pl.multiple_of(n * (groups_per_tile * GROUP_PITCH), 8)
            for g in range(groups_per_tile):
                dst = pl.ds(base + g * GROUP_PITCH, MAX_DIL)
                osc[0, dst, :] = out[g * MAX_DIL:(g + 1) * MAX_DIL]
                lsc[0, dst, :] = lse[g * MAX_DIL:(g + 1) * MAX_DIL]
            return c

        lax.fori_loop(0, SUPER // B_TILE, d1, 0, unroll=16)

        def groups(ref, c):
            return jnp.concatenate(
                [ref[c, g * GROUP_PITCH:g * GROUP_PITCH + MAX_DIL, :] for g in range(SUPER // MAX_DIL)], axis=0)

        l0, l1, l2 = groups(lsc, 0), groups(lsc, 1), groups(lsc, 2)
        m = jnp.maximum(jnp.maximum(l0, l1), l2)
        e0, e1, e2 = jnp.exp2(l0 - m), jnp.exp2(l1 - m), jnp.exp2(l2 - m)
        mixed = (e0 * groups(osc, 0) + e1 * groups(osc, 1) + e2 * groups(osc, 2)) / (e0 + e1 + e2)
        o_ref[0, 0, pl.ds(t0, SUPER), :] = mixed.astype(BF16)
        return carry

    lax.fori_loop(0, seq // SUPER, super_tile, 0)


def _attn_b(qkv, qkv_resid, biases):
    bsz, _, seq, _ = qkv[0].shape
    rows = seq // MAX_DIL
    spec = pl.BlockSpec((1, 1, seq, LANES), lambda hp, b: (b, hp, 0, 0))
    rspec = pl.BlockSpec((1, 1, MAX_DIL, rows, LANES), lambda hp, b: (b, hp, 0, 0, 0))
    bias_specs = [pl.BlockSpec((1,) + t.shape[1:], lambda hp, b: (hp, 0, 0, 0)) for t in biases]
    return pl.pallas_call(
        functools.partial(_attn_b_kernel, seq=seq),
        grid=(B_PAIRS, bsz),
        in_specs=[spec] * 3 + [rspec] * 3 + bias_specs,
        out_specs=spec,
        out_shape=jax.ShapeDtypeStruct((bsz, B_PAIRS, seq, LANES), BF16),
        scratch_shapes=[
            pltpu.VMEM((3, SUPER // MAX_DIL * GROUP_PITCH, LANES), F32),
            pltpu.VMEM((3, SUPER // MAX_DIL * GROUP_PITCH, LANES), F32),
        ],
        compiler_params=_params(2),
        name="attn_b",
    )(*qkv, *qkv_resid, *biases)


def _attn_b_biases(seq):
    slopes = _alibi_slopes(B_HEADS)
    ar = jnp.arange
    pairs = lambda make: jnp.stack([make(slopes[2 * hp:2 * hp + 2]) for hp in range(B_PAIRS)])
    b1 = pairs(lambda sl: _band_bias(sl, ar(B_TILE), ar(B_KEYS), (0, B_RADIUS, 2 * B_RADIUS), B_RADIUS, 1))
    a4 = ar(4)[:, None]
    q4 = (4 * ar(32)[None, :] + a4).reshape(-1)
    k4 = (4 * ar(64)[None, :] + a4).reshape(-1)
    b4 = pairs(lambda sl: _band_bias(sl, q4, k4, (0, 64, 128), B_RADIUS, 4))
    if seq // MAX_DIL == B_TILE:
        b16 = pairs(lambda sl: _band_bias(sl, ar(B_TILE), ar(B_TILE), (0,), B_RADIUS, 16))
    else:
        b16 = pairs(lambda sl: _band_bias(sl, ar(B_TILE), ar(B_KEYS), (0, B_RADIUS, 2 * B_RADIUS), B_RADIUS, 16))
    return b1, b4, b16


def _merge_ln_kernel(sink_ref, x_ref, qa_ref, ka_ref, va_ref, bias_a_ref, ob_ref, km_ref, vm_ref,
                     wqm_ref, wg_ref, wbr_ref, wo_ref, g_ref, b_ref, o_ref, *, seq):
    x = x_ref[0]
    xb = x.astype(BF16)
    low_half = lax.broadcasted_iota(jnp.int32, (1, LANES), 1) < HEAD_DIM
    blocks = []
    for j in range(MERGE_ROWS // A_BLK):
        start = pl.program_id(1) * MERGE_ROWS + j * A_BLK
        window = pl.multiple_of(jnp.clip(start - A_BLK, 0, seq - A_KEYS), A_BLK)
        blocks.append(_attn_a_block(
            qa_ref[0, j * A_BLK:(j + 1) * A_BLK, :], ka_ref[0, pl.ds(window, A_KEYS), :],
            va_ref[0, pl.ds(window, A_KEYS), :], bias_a_ref[_edge_variant(start, seq - A_BLK)],
            sink_ref, low_half))
    oa = jnp.concatenate(blocks, axis=0)
    qm = _dot(xb, wqm_ref[...]).astype(BF16)
    heads = []
    for h in range(M_HEADS):
        lanes = slice(h * M_HEAD_DIM, (h + 1) * M_HEAD_DIM)
        s = _dot_nt(qm[:, lanes], km_ref[0, :, lanes]) * (LOG2E * M_HEAD_DIM ** -0.5)
        p = jnp.exp2(s - jnp.max(s, axis=-1, keepdims=True))
        denom = jnp.sum(p, axis=-1, keepdims=True)
        heads.append(_dot(p.astype(BF16), vm_ref[0, :, lanes]) / denom)
    om = jnp.concatenate(heads, axis=-1).astype(BF16)
    ob = jnp.concatenate([ob_ref[0, hp] for hp in range(B_PAIRS)], axis=-1)
    mixed = None
    for i, branch in enumerate((oa, ob, om)):
        gate = jax.nn.sigmoid(_dot(xb, wg_ref[:, i * D_MODEL:(i + 1) * D_MODEL]))
        term = gate * _dot(branch, wbr_ref[i])
        mixed = term if mixed is None else mixed + term
    y = ALPHA * x + _dot(mixed.astype(BF16), wo_ref[...])
    o_ref[0] = _layer_norm(y, g_ref[...], b_ref[...])


def _merge_ln(sink, x, qa, ka, va, bias_a, ob, km, vm, w_qm, w_gate, w_branch, w_out, g, b):
    bsz, seq, _ = x.shape
    rows = MERGE_ROWS
    tok = lambda cols: pl.BlockSpec((1, rows, cols), lambda bi, i: (bi, i, 0))
    whole = pl.BlockSpec((1, seq, LANES), lambda bi, i: (bi, 0, 0))
    mem = pl.BlockSpec((1, N_MEM, QM_COLS), lambda bi, i: (bi, 0, 0))
    return pl.pallas_call(
        functools.partial(_merge_ln_kernel, seq=seq),
        grid=(bsz, seq // rows),
        in_specs=[pl.BlockSpec(memory_space=pltpu.SMEM), tok(D_MODEL), tok(QA_COLS), whole, whole,
                  _resident(bias_a.shape),
                  pl.BlockSpec((1, B_PAIRS, rows, LANES), lambda bi, i: (bi, 0, i, 0)),
                  mem, mem,
                  _resident(w_qm.shape), _resident(w_gate.shape), _resident(w_branch.shape),
                  _resident(w_out.shape), _resident((1, D_MODEL)), _resident((1, D_MODEL))],
        out_specs=tok(D_MODEL),
        out_shape=jax.ShapeDtypeStruct((bsz, seq, D_MODEL), F32),
        compiler_params=_params(2),
        name="merge_ln",
    )(sink, x, qa, ka, va, bias_a, ob, km, vm, w_qm, w_gate, w_branch, w_out, g, b)


def _slab_order():
    cols = []
    for i in range(A_GROUP):
        cols += list(range(i * HEAD_DIM, (i + 1) * HEAD_DIM))
        cols += list(range((A_GROUP + i) * HEAD_DIM, (A_GROUP + i + 1) * HEAD_DIM))
    return jnp.asarray(cols, dtype=jnp.int32)


def kernel(x_prompt, x_sample, mem_prompt, mem_sample, ffn1_w_in, ffn1_w_out, ln1_g, ln1_b, w_in, w_mem_kv,
           sink_a, w_branch, w_out, ln2_g, ln2_b, ffn2_w_in, ffn2_w_out, ln3_g, ln3_b):
    order = _slab_order()
    bias_a = _attn_a_bias()
    q_scale = LOG2E * HEAD_DIM ** -0.5

    layers = []
    for l in range(DEPTH):
        w = w_in[l]
        qa_w = w[:, :QA_COLS][:, order] * q_scale
        kva_w = w[:, QA_COLS:QA_COLS + KVA_COLS]
        qb_w = w[:, QA_COLS + KVA_COLS:QA_COLS + KVA_COLS + BRANCH_WIDTH] * q_scale
        kvb_w = w[:, QA_COLS + KVA_COLS + BRANCH_WIDTH:QM_START]
        w_proj = jnp.concatenate([qa_w, kva_w, qb_w, kvb_w], axis=1).astype(BF16)
        w_br = jnp.concatenate([w_branch[l, :1][:, order], w_branch[l, 1:]], axis=0).astype(BF16)
        row = lambda t: t[l].reshape(1, D_MODEL).astype(F32)
        layers.append(dict(
            ffn1=(ffn1_w_in[l].astype(BF16), (0.5 * ffn1_w_out[l]).astype(BF16), row(ln1_g), row(ln1_b)),
            ffn2=(ffn2_w_in[l].astype(BF16), (0.5 * ffn2_w_out[l]).astype(BF16), row(ln3_g), row(ln3_b)),
            w_proj=w_proj,
            w_mem=w_mem_kv[l].astype(BF16),
            sink=sink_a[l].astype(F32) * LOG2E,
            merge=(w[:, QM_START:GATE_START].astype(BF16), w[:, GATE_START:].astype(BF16), w_br,
                   w_out[l].astype(BF16), row(ln2_g), row(ln2_b)),
        ))

    def trunk(x, mem):
        bsz, seq, _ = x.shape
        biases_b = _attn_b_biases(seq)
        for p in layers:
            x, qa, ka, va, *qkv_b = _ffn_ln_proj(x, *p["ffn1"], p["w_proj"])
            km, vm = _mem_kv(mem, p["w_mem"])
            ob = _attn_b(qkv_b[:3], qkv_b[3:], biases_b)
            x = _merge_ln(p["sink"], x, qa, ka, va, bias_a, ob, km, vm, *p["merge"])
            x = _ffn_ln(x.reshape(bsz * seq, D_MODEL), *p["ffn2"]).reshape(bsz, seq, D_MODEL)
        return x

    return trunk(x_prompt, mem_prompt), trunk(x_sample, mem_sample)
```
